```python
import jax, jax.numpy as jnp
from jax import lax
import numpy as np

D_MODEL = 1024
BATCH = 4
SEQ = 8192
DEPTH = 4

GRID_W = 64
CTX_LEN = 256
POOL_GROUPS = 4
POOL_WINDOWS = (2, 4, 8, 16)
POOL_WIDTH = D_MODEL // 4
FOURIER_GROUPS = 4
FOURIER_WIDTH = D_MODEL // 4
HEAD_DIM = 64
N_HEADS = (D_MODEL // 2) // HEAD_DIM
ATTN_WIDTH = N_HEADS * HEAD_DIM
MIX_WIDTH = POOL_WIDTH + FOURIER_WIDTH + ATTN_WIDTH
P_END = POOL_WIDTH
F_END = P_END + FOURIER_WIDTH
Q_END = F_END + ATTN_WIDTH
K_END = Q_END + ATTN_WIDTH
IN_WIDTH = K_END + ATTN_WIDTH
WIN_R = 8
WIN_C = 16
D_FF = -(-8 * D_MODEL // (3 * 256)) * 256
EPS = 1e-6
NEG_INF = -1e30

kernel_name = "hybrid_pool_fourier_natten_dit"


def rms_norm(x, g):
    xf = x.astype(jnp.float32)
    y = xf * lax.rsqrt(jnp.mean(xf * xf, axis=-1, keepdims=True) + EPS)
    return (y * g.astype(jnp.float32)).astype(x.dtype)


def modulate(h, shift, scale):
    return h * (1 + scale) + shift


def ada_mod(cond, w, b):
    m = jax.nn.silu(cond) @ w + b
    return jnp.split(m, 6, axis=-1)


def split_heads(t):
    return t.reshape(t.shape[0], t.shape[1], N_HEADS, HEAD_DIM)


def multiscale_pool(u, pool_w, pool_scale):
    b, s, _ = u.shape
    gw = POOL_WIDTH // POOL_GROUPS
    ug = u.reshape(b, s, POOL_GROUPS, gw).astype(jnp.float32)
    csum = jnp.cumsum(ug, axis=1)
    csum = jnp.concatenate([jnp.zeros_like(csum[:, :1]), csum], axis=1)
    t = jnp.arange(s)
    means = []
    for g, w in enumerate(POOL_WINDOWS):
        lo = jnp.clip(t - w // 2, 0, s)
        hi = jnp.clip(t + w - w // 2, 0, s)
        cnt = (hi - lo).astype(jnp.float32)
        means.append((csum[:, hi, g] - csum[:, lo, g]) / cnt[None, :, None])
    diff = (jnp.stack(means, axis=2) - ug).astype(u.dtype)
    y = jnp.einsum('bsgc,gcd->bsgd', diff, pool_w).reshape(b, s, POOL_WIDTH)
    return y * pool_scale


def fourier_mix(u, fourier_w):
    b, s, _ = u.shape
    ug = u.reshape(b, s, FOURIER_GROUPS, FOURIER_WIDTH // FOURIER_GROUPS).astype(jnp.float32)
    f = jnp.fft.fftn(ug, axes=(1, 3), norm="ortho").real.astype(u.dtype)
    return jnp.einsum('bsgc,gcd->bsgd', f, fourier_w).reshape(b, s, FOURIER_WIDTH)


def context_attention(q, k, v):
    b, n = q.shape[:2]
    s = jnp.einsum('bqhd,bkhd->bhqk', q, k).astype(jnp.float32) * (HEAD_DIM ** -0.5)
    p = jax.nn.softmax(s, axis=-1).astype(v.dtype)
    return jnp.einsum('bhqk,bkhd->bqhd', p, v).reshape(b, n, ATTN_WIDTH)


def neighbourhood_attention(q, k, v, k_ctx, v_ctx, bias_table):
    b, s = q.shape[:2]
    rows = s // GRID_W
    kr = min(WIN_R, rows)
    qg = q.reshape(b, rows, GRID_W, N_HEADS, HEAD_DIM)
    kg = k.reshape(b, rows, GRID_W, N_HEADS, HEAD_DIM)
    vg = v.reshape(b, rows, GRID_W, N_HEADS, HEAD_DIM)
    col = jnp.arange(GRID_W)
    cs = jnp.clip(col - WIN_C // 2, 0, GRID_W - WIN_C)
    col_mask = (col[None, :] >= cs[:, None]) & (col[None, :] < cs[:, None] + WIN_C)
    dc_idx = jnp.clip(col[None, :] - col[:, None] + WIN_C - 1, 0, 2 * WIN_C - 2)
    scale = HEAD_DIM ** -0.5
    n_loc = kr * GRID_W

    def row_block(r):
        rs = jnp.clip(r - kr // 2, 0, rows - kr)
        q_r = lax.dynamic_index_in_dim(qg, r, axis=1, keepdims=False)
        k_b = lax.dynamic_slice_in_dim(kg, rs, kr, axis=1)
        v_b = lax.dynamic_slice_in_dim(vg, rs, kr, axis=1)
        dr_idx = rs + jnp.arange(kr) - r + WIN_R - 1
        bias = bias_table[:, dr_idx][:, :, dc_idx].transpose(0, 2, 1, 3)
        s_loc = jnp.einsum('bchd,bijhd->bhcij', q_r, k_b).astype(jnp.float32) * scale
        s_loc = jnp.where(col_mask[None, None, :, None, :],
                          s_loc + bias.astype(jnp.float32)[None], NEG_INF)
        s_ctx = jnp.einsum('bchd,bkhd->bhck', q_r, k_ctx).astype(jnp.float32) * scale
        scores = jnp.concatenate([s_loc.reshape(b, N_HEADS, GRID_W, n_loc), s_ctx], axis=-1)
        p = jax.nn.softmax(scores, axis=-1).astype(v.dtype)
        p_loc = p[..., :n_loc].reshape(b, N_HEADS, GRID_W, kr, GRID_W)
        p_ctx = p[..., n_loc:]
        return (jnp.einsum('bhcij,bijhd->bchd', p_loc, v_b)
                + jnp.einsum('bhck,bkhd->bchd', p_ctx, v_ctx))

    o = lax.map(row_block, jnp.arange(rows))
    return jnp.moveaxis(o, 0, 1).reshape(b, s, ATTN_WIDTH)


def token_mixers(u, attn_out, pool_w, pool_scale, fourier_w, w_out):
    y = jnp.concatenate([multiscale_pool(u[..., :P_END], pool_w, pool_scale),
                         fourier_mix(u[..., P_END:F_END], fourier_w),
                         attn_out], axis=-1)
    return y @ w_out


def swiglu(h, w1, w3, w2):
    return (jax.nn.silu(h @ w1) * (h @ w3)) @ w2


def setup_inputs(seed: int = 0) -> dict:
    key = jax.random.key(seed)
    ks = jax.random.split(key, 20)
    nrm = jax.random.normal
    f32 = jnp.float32
    gw = POOL_WIDTH // POOL_GROUPS
    fw = FOURIER_WIDTH // FOURIER_GROUPS
    return {
        "x": nrm(ks[0], (BATCH, SEQ, D_MODEL), f32),
        "c": nrm(ks[1], (BATCH, D_MODEL), f32),
        "ctx": nrm(ks[2], (BATCH, CTX_LEN, D_MODEL), f32),
        "c_ctx": nrm(ks[3], (D_MODEL,), f32),
        "w_ada": nrm(ks[4], (DEPTH, D_MODEL, 6 * D_MODEL), f32) * (0.5 * D_MODEL ** -0.5),
        "b_ada": nrm(ks[5], (DEPTH, 6 * D_MODEL), f32) * 0.01,
        "norm1_g": 1.0 + 0.05 * nrm(ks[6], (DEPTH, D_MODEL), f32),
        "w_in": nrm(ks[7], (DEPTH, D_MODEL, IN_WIDTH), f32) * D_MODEL ** -0.5,
        "pool_w": nrm(ks[8], (DEPTH, POOL_GROUPS, gw, gw), f32) * gw ** -0.5,
        "pool_scale": 1.0 + 0.1 * nrm(ks[9], (DEPTH, POOL_WIDTH), f32),
        "fourier_w": nrm(ks[10], (DEPTH, FOURIER_GROUPS, fw, fw), f32) * fw ** -0.5,
        "nat_bias": 0.1 * nrm(ks[11], (DEPTH, N_HEADS, 2 * WIN_R - 1, 2 * WIN_C - 1), f32),
        "w_out": nrm(ks[12], (DEPTH, MIX_WIDTH, D_MODEL), f32) * MIX_WIDTH ** -0.5,
        "norm2_g": 1.0 + 0.05 * nrm(ks[13], (DEPTH, D_MODEL), f32),
        "w_ffn1": nrm(ks[14], (DEPTH, D_MODEL, D_FF), f32) * D_MODEL ** -0.5,
        "w_ffn3": nrm(ks[15], (DEPTH, D_MODEL, D_FF), f32) * D_MODEL ** -0.5,
        "w_ffn2": nrm(ks[16], (DEPTH, D_FF, D_MODEL), f32) * D_FF ** -0.5,
        "final_g": 1.0 + 0.05 * nrm(ks[17], (D_MODEL,), f32),
    }


def reference(x, c, ctx, c_ctx, w_ada, b_ada, norm1_g, w_in, pool_w, pool_scale, fourier_w,
              nat_bias, w_out, norm2_g, w_ffn1, w_ffn3, w_ffn2, final_g):
    for l in range(DEPTH):
        last = l == DEPTH - 1
        sh1, sc1, g1, sh2, sc2, g2 = [m[:, None, :] for m in ada_mod(c, w_ada[l], b_ada[l])]
        csh1, csc1, cg1, csh2, csc2, cg2 = ada_mod(c_ctx, w_ada[l], b_ada[l])

        hc = modulate(rms_norm(ctx, norm1_g[l]), csh1, csc1)
        if last:
            kvc = hc @ w_in[l][:, Q_END:]
            kc, vc = split_heads(kvc[..., :ATTN_WIDTH]), split_heads(kvc[..., ATTN_WIDTH:])
        else:
            uc = hc @ w_in[l]
            kc, vc = split_heads(uc[..., Q_END:K_END]), split_heads(uc[..., K_END:])

        hx = modulate(rms_norm(x, norm1_g[l]), sh1, sc1)
        ux = hx @ w_in[l]
        attn_x = neighbourhood_attention(split_heads(ux[..., F_END:Q_END]),
                                         split_heads(ux[..., Q_END:K_END]),
                                         split_heads(ux[..., K_END:]), kc, vc, nat_bias[l])
        x = x + g1 * token_mixers(ux, attn_x, pool_w[l], pool_scale[l], fourier_w[l], w_out[l])
        x = x + g2 * swiglu(modulate(rms_norm(x, norm2_g[l]), sh2, sc2),
                            w_ffn1[l], w_ffn3[l], w_ffn2[l])

        if not last:
            attn_c = context_attention(split_heads(uc[..., F_END:Q_END]), kc, vc)
            ctx = ctx + cg1 * token_mixers(uc, attn_c, pool_w[l], pool_scale[l], fourier_w[l], w_out[l])
            ctx = ctx + cg2 * swiglu(modulate(rms_norm(ctx, norm2_g[l]), csh2, csc2),
                                     w_ffn1[l], w_ffn3[l], w_ffn2[l])
    return rms_norm(x, final_g)
```

```python
import functools

import numpy as np
import jax
import jax.numpy as jnp
from jax import lax
from jax.experimental import pallas as pl
from jax.experimental.pallas import tpu as pltpu

D_MODEL = 1024
BATCH = 4
SEQ = 8192
DEPTH = 4
GRID_W = 64
GRID_ROWS = SEQ // GRID_W
CTX_LEN = 256
POOL_WINDOWS = (2, 4, 8, 16)
POOL_WIDTH = 256
FOURIER_WIDTH = 256
GROUP_WIDTH = 64
HEAD_DIM = 64
N_HEADS = 8
N_PAIRS = N_HEADS // 2
ATTN_WIDTH = 512
WIN_R = 8
WIN_C = 16
D_FF = 2816
EPS = 1e-6
NEG_INF = -1e30

FFT_N1 = 128
FFT_N2 = 64
POOL_HALO = 16
N_ROW_CLASSES = 8

V7X_VMEM_LIMIT = 56 * 1024 * 1024

F32 = jnp.float32
BF16 = jnp.bfloat16


def _dot(a, b):
    return jnp.dot(a, b, preferred_element_type=F32)


def _dot_nt(a, b):
    return lax.dot_general(a, b, (((1,), (1,)), ((), ())), preferred_element_type=F32)


def _params(n_grid_dims):
    return pltpu.CompilerParams(dimension_semantics=("arbitrary",) * n_grid_dims,
                                vmem_limit_bytes=V7X_VMEM_LIMIT)


def _resident(block_shape, index_map):
    return pl.BlockSpec(block_shape, index_map, pipeline_mode=pl.Buffered(1))


def _block_diag(m, groups):
    w = m.shape[0]
    out = np.zeros((groups * w, groups * w), m.dtype)
    for g in range(groups):
        out[g * w:(g + 1) * w, g * w:(g + 1) * w] = m
    return out


@functools.lru_cache(maxsize=None)
def _dft_constants():
    two_pi = 2.0 * np.pi
    a2 = two_pi * (np.outer(np.arange(FFT_N2), np.arange(FFT_N2)) % FFT_N2) / FFT_N2
    stage1 = np.concatenate([np.cos(a2), -np.sin(a2)], axis=0)
    k = FFT_N2 * np.arange(FFT_N1)[None, :, None] + np.arange(FFT_N2)[:, None, None]
    n1 = np.arange(FFT_N1)[None, None, :]
    ang = two_pi * ((k * n1) % SEQ) / SEQ
    mr, mi = np.cos(ang), -np.sin(ang)
    stage2 = np.concatenate([np.concatenate([mr, -mi], axis=2),
                             np.concatenate([mi, mr], axis=2)], axis=1)
    ac = two_pi * (np.outer(np.arange(CTX_LEN), np.arange(CTX_LEN)) % CTX_LEN) / CTX_LEN
    ctx_dft = np.concatenate([np.cos(ac), -np.sin(ac)], axis=0)
    ag = two_pi * (np.outer(np.arange(GROUP_WIDTH), np.arange(GROUP_WIDTH)) % GROUP_WIDTH) / GROUP_WIDTH
    chan = np.concatenate([_block_diag(np.cos(ag), 4), _block_diag(np.sin(ag), 4)], axis=0)
    as_bf16 = lambda a: jnp.asarray(a, dtype=F32).astype(BF16)
    return as_bf16(stage1), as_bf16(stage2), as_bf16(ctx_dft), as_bf16(chan)


def _group_block_diag(w):
    out = jnp.zeros((4, GROUP_WIDTH, 4, GROUP_WIDTH), w.dtype)
    for g in range(4):
        out = out.at[g, :, g, :].set(w[g])
    return out.reshape(4 * GROUP_WIDTH, 4 * GROUP_WIDTH).astype(BF16)


def _attention_bias(table):
    col = np.arange(GRID_W)
    cs = np.clip(col - WIN_C // 2, 0, GRID_W - WIN_C)
    col_mask = (col[None, :] >= cs[:, None]) & (col[None, :] < cs[:, None] + WIN_C)
    dc_idx = np.clip(col[None, :] - col[:, None] + WIN_C - 1, 0, 2 * WIN_C - 2)
    rep_rows = np.array([0, 1, 2, 3, 4, GRID_ROWS - 3, GRID_ROWS - 2, GRID_ROWS - 1])
    rs = np.clip(rep_rows - WIN_R // 2, 0, GRID_ROWS - WIN_R)
    dr_idx = rs[:, None] + np.arange(WIN_R)[None, :] - rep_rows[:, None] + WIN_R - 1
    b = table[:, dr_idx]
    b = b[:, :, :, dc_idx]
    b = jnp.where(col_mask[None, None, None], b.astype(F32), NEG_INF)
    b = b.transpose(1, 0, 3, 2, 4)
    return b.reshape(N_ROW_CLASSES, N_PAIRS, 2 * GRID_W, WIN_R * GRID_W)


def _ada_kernel(c_ref, w_ref, b_ref, o_ref):
    c = c_ref[...]
    a = (c * jax.nn.sigmoid(c)).astype(BF16)
    o_ref[0] = _dot(a, w_ref[0].astype(BF16)) + b_ref[0]


def _ada_mod(cond, w_ada, b_ada):
    tn = 1536
    out = pl.pallas_call(
        _ada_kernel,
        grid=(DEPTH, 6 * D_MODEL // tn),
        in_specs=[_resident((8, D_MODEL), lambda l, j: (0, 0)),
                  pl.BlockSpec((1, D_MODEL, tn), lambda l, j: (l, 0, j)),
                  pl.BlockSpec((1, 1, tn), lambda l, j: (l, 0, j))],
        out_specs=pl.BlockSpec((1, 8, tn), lambda l, j: (l, 0, j)),
        out_shape=jax.ShapeDtypeStruct((DEPTH, 8, 6 * D_MODEL), F32),
        compiler_params=_params(2),
        name="ada_mod",
    )(cond, w_ada, b_ada.reshape(DEPTH, 1, 6 * D_MODEL))
    return out.reshape(DEPTH, 8, 6, D_MODEL)


def _rms(x):
    return x * lax.rsqrt(jnp.mean(x * x, axis=-1, keepdims=True) + EPS)


def _inproj_kernel(x_ref, mod_ref, g_ref, w_ref, up_ref, uf_ref, qkv_ref):
    y = _rms(x_ref[...]) * g_ref[...]
    h = (y * (1.0 + mod_ref[0, 1:2, :]) + mod_ref[0, 0:1, :]).astype(BF16)
    up_ref[...] = _dot(h, w_ref[:, 0:256]).astype(BF16)
    uf_ref[...] = _dot(h, w_ref[:, 256:512]).astype(BF16)
    qkv_ref[:, 0:512] = (_dot(h, w_ref[:, 512:1024]) * (HEAD_DIM ** -0.5)).astype(BF16)
    qkv_ref[:, 512:1024] = _dot(h, w_ref[:, 1024:1536]).astype(BF16)
    qkv_ref[:, 1024:1536] = _dot(h, w_ref[:, 1536:2048]).astype(BF16)


def _inproj(x, mod, g, w_in, mod_row, tm):
    n = x.shape[0]
    return pl.pallas_call(
        _inproj_kernel,
        grid=(n // tm,),
        in_specs=[pl.BlockSpec((tm, D_MODEL), lambda i: (i, 0)),
                  pl.BlockSpec((1, 6, D_MODEL), lambda i: (mod_row(i, tm), 0, 0)),
                  _resident((1, D_MODEL), lambda i: (0, 0)),
                  _resident((D_MODEL, 2048), lambda i: (0, 0))],
        out_specs=[pl.BlockSpec((tm, 256), lambda i: (i, 0)),
                   pl.BlockSpec((tm, 256), lambda i: (i, 0)),
                   pl.BlockSpec((tm, 1536), lambda i: (i, 0))],
        out_shape=[jax.ShapeDtypeStruct((n, 256), BF16),
                   jax.ShapeDtypeStruct((n, 256), BF16),
                   jax.ShapeDtypeStruct((n, 1536), BF16)],
        compiler_params=_params(1),
        name="inproj",
    )(x, mod, g, w_in)


def _pool_kernel(prev_ref, cur_ref, next_ref, pw_ref, ps_ref, o_ref, *, seq_tiles):
    tm = cur_ref.shape[0]
    st = pl.program_id(0) % seq_tiles
    lo_b = jnp.where(st == 0, 0, -POOL_HALO)
    hi_b = jnp.where(st == seq_tiles - 1, tm, tm + POOL_HALO)
    cur = cur_ref[...]
    ub = jnp.concatenate([prev_ref[...], cur, next_ref[...]], axis=0)
    t = lax.broadcasted_iota(jnp.int32, (tm, tm + 2 * POOL_HALO), 0)
    p = lax.broadcasted_iota(jnp.int32, (tm, tm + 2 * POOL_HALO), 1) - POOL_HALO
    d = p - t
    valid = (p >= lo_b) & (p < hi_b)
    tcol = lax.broadcasted_iota(jnp.int32, (tm, 1), 0)
    group = lax.broadcasted_iota(jnp.int32, (tm, POOL_WIDTH), 1) // GROUP_WIDTH
    mean = jnp.zeros((tm, POOL_WIDTH), F32)
    for g, w in enumerate(POOL_WINDOWS):
        band = valid & (d >= -(w // 2)) & (d < w - w // 2)
        a = jnp.where(band, 1.0, 0.0).astype(BF16)
        s = _dot(a, ub)
        cnt = jnp.minimum(tcol + (w - w // 2), hi_b) - jnp.maximum(tcol - w // 2, lo_b)
        mean = jnp.where(group == g, s / cnt.astype(F32), mean)
    diff = (mean - cur.astype(F32)).astype(BF16)
    o_ref[...] = (_dot(diff, pw_ref[...]) * ps_ref[...]).astype(BF16)


def _pool(up, pool_w_bd, pool_scale, seq_len):
    n = up.shape[0]
    tm = 256
    hb = tm // POOL_HALO
    n_halo_blocks = n // POOL_HALO
    return pl.pallas_call(
        functools.partial(_pool_kernel, seq_tiles=seq_len // tm),
        grid=(n // tm,),
        in_specs=[pl.BlockSpec((POOL_HALO, POOL_WIDTH), lambda i: (jnp.maximum(i * hb - 1, 0), 0)),
                  pl.BlockSpec((tm, POOL_WIDTH), lambda i: (i, 0)),
                  pl.BlockSpec((POOL_HALO, POOL_WIDTH),
                               lambda i: (jnp.minimum((i + 1) * hb, n_halo_blocks - 1), 0)),
                  _resident((POOL_WIDTH, POOL_WIDTH), lambda i: (0, 0)),
                  _resident((1, POOL_WIDTH), lambda i: (0, 0))],
        out_specs=pl.BlockSpec((tm, POOL_WIDTH), lambda i: (i, 0)),
        out_shape=jax.ShapeDtypeStruct((n, POOL_WIDTH), BF16),
        compiler_params=_params(1),
        name="pool",
    )(up, up, up, pool_w_bd, pool_scale)


def _fft1_kernel(m_ref, x_ref, y_ref):
    y_ref[0] = _dot(m_ref[...], x_ref[...]).astype(BF16)


def _fourier_epilogue(zr, zi, cs_ref, w_ref, norm):
    zz = jnp.concatenate([zr, zi], axis=1).astype(BF16)
    f = (_dot(zz, cs_ref[...]) * norm).astype(BF16)
    return _dot(f, w_ref[...]).astype(BF16)


def _fft2_kernel(t_ref, y_ref, cs_ref, w_ref, o_ref, *, kb, norm):
    for kk in range(kb):
        yk = jnp.concatenate([y_ref[0, 0, kk], y_ref[0, 1, kk]], axis=0)
        z = _dot(t_ref[kk], yk)
        o_ref[:, kk * FOURIER_WIDTH:(kk + 1) * FOURIER_WIDTH] = _fourier_epilogue(
            z[:FFT_N1], z[FFT_N1:], cs_ref, w_ref, norm)


def _fourier_latent(uf, fourier_w_bd):
    stage1, stage2, _, chan = _dft_constants()
    cols = FFT_N1 * FOURIER_WIDTH
    tc = 8192
    x = uf.reshape(BATCH * FFT_N2, cols)
    y = pl.pallas_call(
        _fft1_kernel,
        grid=(BATCH, cols // tc),
        in_specs=[_resident((2 * FFT_N2, FFT_N2), lambda b, j: (0, 0)),
                  pl.BlockSpec((FFT_N2, tc), lambda b, j: (b, j))],
        out_specs=pl.BlockSpec((1, 2 * FFT_N2, tc), lambda b, j: (b, 0, j)),
        out_shape=jax.ShapeDtypeStruct((BATCH, 2 * FFT_N2, cols), BF16),
        compiler_params=_params(2),
        name="fft_stage1",
    )(stage1, x)
    y = y.reshape(BATCH, 2, FFT_N2, FFT_N1, FOURIER_WIDTH)
    kb = 8
    out = pl.pallas_call(
        functools.partial(_fft2_kernel, kb=kb, norm=float((SEQ * GROUP_WIDTH) ** -0.5)),
        grid=(FFT_N2 // kb, BATCH),
        in_specs=[pl.BlockSpec((kb, 2 * FFT_N1, 2 * FFT_N1), lambda j, b: (j, 0, 0)),
                  pl.BlockSpec((1, 2, kb, FFT_N1, FOURIER_WIDTH), lambda j, b: (b, 0, j, 0, 0)),
                  _resident((2 * FOURIER_WIDTH, FOURIER_WIDTH), lambda j, b: (0, 0)),
                  _resident((FOURIER_WIDTH, FOURIER_WIDTH), lambda j, b: (0, 0))],
        out_specs=pl.BlockSpec((FFT_N1, kb * FOURIER_WIDTH), lambda j, b: (b, j)),
        out_shape=jax.ShapeDtypeStruct((BATCH * FFT_N1, FFT_N2 * FOURIER_WIDTH), BF16),
        compiler_params=_params(2),
        name="fft_stage2",
    )(stage2, y, chan, fourier_w_bd)
    return out.reshape(BATCH * SEQ, FOURIER_WIDTH)


def _fourier_ctx_kernel(m_ref, u_ref, cs_ref, w_ref, o_ref, *, norm):
    z = _dot(m_ref[...], u_ref[...])
    o_ref[...] = _fourier_epilogue(z[:CTX_LEN], z[CTX_LEN:], cs_ref, w_ref, norm)


def _fourier_ctx(ufc, fourier_w_bd):
    _, _, ctx_dft, chan = _dft_constants()
    return pl.pallas_call(
        functools.partial(_fourier_ctx_kernel, norm=float((CTX_LEN * GROUP_WIDTH) ** -0.5)),
        grid=(BATCH,),
        in_specs=[_resident((2 * CTX_LEN, CTX_LEN), lambda b: (0, 0)),
                  pl.BlockSpec((CTX_LEN, FOURIER_WIDTH), lambda b: (b, 0)),
                  _resident((2 * FOURIER_WIDTH, FOURIER_WIDTH), lambda b: (0, 0)),
                  _resident((FOURIER_WIDTH, FOURIER_WIDTH), lambda b: (0, 0))],
        out_specs=pl.BlockSpec((CTX_LEN, FOURIER_WIDTH), lambda b: (b, 0)),
        out_shape=jax.ShapeDtypeStruct((BATCH * CTX_LEN, FOURIER_WIDTH), BF16),
        compiler_params=_params(1),
        name="fourier_ctx",
    )(ctx_dft, ufc, chan, fourier_w_bd)


def _split_pair(x, even):
    zero = jnp.zeros_like(x)
    return jnp.concatenate([jnp.where(even, x, zero), jnp.where(even, zero, x)], axis=0)


def _softmax_pv(scores, values, n_q, even):
    m = functools.reduce(jnp.maximum, [jnp.max(s, axis=1, keepdims=True) for s in scores])
    e = [jnp.exp(s - m) for s in scores]
    l = functools.reduce(jnp.add, [jnp.sum(x, axis=1, keepdims=True) for x in e])
    o = functools.reduce(jnp.add, [_dot(x.astype(BF16), v) for x, v in zip(e, values)])
    o = o / l
    return jnp.where(even, o[:n_q], o[n_q:])


ATTN_TILE_ROWS = 8
ATTN_TILE = ATTN_TILE_ROWS * GRID_W


def _nat_kernel(q_ref, kp_ref, kc_ref, kn_ref, vp_ref, vc_ref, vn_ref, kx_ref, vx_ref, bias_ref,
                o_ref, kbuf, vbuf):
    t = pl.program_id(1)
    kbuf[0:ATTN_TILE] = kp_ref[...]
    kbuf[ATTN_TILE:2 * ATTN_TILE] = kc_ref[...]
    kbuf[2 * ATTN_TILE:3 * ATTN_TILE] = kn_ref[...]
    vbuf[0:ATTN_TILE] = vp_ref[...]
    vbuf[ATTN_TILE:2 * ATTN_TILE] = vc_ref[...]
    vbuf[2 * ATTN_TILE:3 * ATTN_TILE] = vn_ref[...]
    even = lax.broadcasted_iota(jnp.int32, (GRID_W, 128), 1) < HEAD_DIM
    n_win = WIN_R * GRID_W

    def row(j, carry):
        r = t * ATTN_TILE_ROWS + j
        rs = jnp.clip(r - WIN_R // 2, 0, GRID_ROWS - WIN_R)
        off = pl.multiple_of((rs - (t - 1) * ATTN_TILE_ROWS) * GRID_W, GRID_W)
        cls = jnp.where(r < 4, r, jnp.where(r > GRID_ROWS - 4, r - (GRID_ROWS - N_ROW_CLASSES), 4))
        qoff = pl.multiple_of(j * GRID_W, GRID_W)
        for p in range(N_PAIRS):
            lanes = slice(128 * p, 128 * (p + 1))
            qs = _split_pair(q_ref[pl.ds(qoff, GRID_W), lanes], even)
            s_loc = _dot_nt(qs, kbuf[pl.ds(off, n_win), lanes]) + bias_ref[cls, p]
            s_ctx = _dot_nt(qs, kx_ref[:, lanes])
            o = _softmax_pv([s_loc, s_ctx], [vbuf[pl.ds(off, n_win), lanes], vx_ref[:, lanes]],
                            GRID_W, even)
            o_ref[pl.ds(qoff, GRID_W), lanes] = o.astype(BF16)
        return carry

    lax.fori_loop(0, ATTN_TILE_ROWS, row, 0)


def _neighbourhood_attention(qkv, qkv_ctx, bias):
    tiles = SEQ // ATTN_TILE
    cur = lambda col: (lambda b, t: (b * tiles + t, col))
    prev = lambda col: (lambda b, t: (b * tiles + jnp.maximum(t - 1, 0), col))
    nxt = lambda col: (lambda b, t: (b * tiles + jnp.minimum(t + 1, tiles - 1), col))
    blk = lambda f: pl.BlockSpec((ATTN_TILE, ATTN_WIDTH), f)
    return pl.pallas_call(
        _nat_kernel,
        grid=(BATCH, tiles),
        in_specs=[blk(cur(0)), blk(prev(1)), blk(cur(1)), blk(nxt(1)),
                  blk(prev(2)), blk(cur(2)), blk(nxt(2)),
                  pl.BlockSpec((CTX_LEN, ATTN_WIDTH), lambda b, t: (b, 1)),
                  pl.BlockSpec((CTX_LEN, ATTN_WIDTH), lambda b, t: (b, 2)),
                  _resident(bias.shape, lambda b, t: (0, 0, 0, 0))],
        out_specs=blk(cur(0)),
        out_shape=jax.ShapeDtypeStruct((BATCH * SEQ, ATTN_WIDTH), BF16),
        scratch_shapes=[pltpu.VMEM((3 * ATTN_TILE, ATTN_WIDTH), BF16),
                        pltpu.VMEM((3 * ATTN_TILE, ATTN_WIDTH), BF16)],
        compiler_params=_params(2),
        name="nat_attention",
    )(qkv, qkv, qkv, qkv, qkv, qkv, qkv, qkv_ctx, qkv_ctx, bias)


def _ctx_attn_kernel(q_ref, k_ref, v_ref, o_ref):
    even = lax.broadcasted_iota(jnp.int32, (CTX_LEN, 128), 1) < HEAD_DIM
    for p in range(N_PAIRS):
        lanes = slice(128 * p, 128 * (p + 1))
        qs = _split_pair(q_ref[:, lanes], even)
        s = _dot_nt(qs, k_ref[:, lanes])
        o_ref[:, lanes] = _softmax_pv([s], [v_ref[:, lanes]], CTX_LEN, even).astype(BF16)


def _context_attention(qkv_ctx):
    blk = lambda col: pl.BlockSpec((CTX_LEN, ATTN_WIDTH), lambda b: (b, col))
    return pl.pallas_call(
        _ctx_attn_kernel,
        grid=(BATCH,),
        in_specs=[blk(0), blk(1), blk(2)],
        out_specs=blk(0),
        out_shape=jax.ShapeDtypeStruct((BATCH * CTX_LEN, ATTN_WIDTH), BF16),
        compiler_params=_params(1),
        name="ctx_attention",
    )(qkv_ctx, qkv_ctx, qkv_ctx)


FFN_CHUNK = D_FF // 2


def _outffn_kernel(x_ref, yp_ref, yf_ref, at_ref, mod_ref, wo_ref, g2_ref, w1_ref, w3_ref, w2_ref,
                   *rest, final):
    o_ref = rest[-1]
    y = jnp.concatenate([yp_ref[...], yf_ref[...], at_ref[...]], axis=1)
    x1 = x_ref[...] + mod_ref[0, 2:3, :] * _dot(y, wo_ref[...])
    h = (_rms(x1) * g2_ref[...] * (1.0 + mod_ref[0, 4:5, :]) + mod_ref[0, 3:4, :]).astype(BF16)
    ffn = jnp.zeros_like(x1)
    for c in range(D_FF // FFN_CHUNK):
        cols = slice(c * FFN_CHUNK, (c + 1) * FFN_CHUNK)
        a = _dot(h, w1_ref[:, cols])
        gated = (a * jax.nn.sigmoid(a) * _dot(h, w3_ref[:, cols])).astype(BF16)
        ffn = ffn + _dot(gated, w2_ref[cols, :])
    x2 = x1 + mod_ref[0, 5:6, :] * ffn
    if final:
        x2 = _rms(x2) * rest[0][...]
    o_ref[...] = x2


def _outffn(x, yp, yf, at, mod, w_out, g2, w1, w3, w2, final_g, mod_row, tm):
    n = x.shape[0]
    row = lambda w: pl.BlockSpec((tm, w), lambda i: (i, 0))
    in_specs = [row(D_MODEL), row(256), row(256), row(512),
                pl.BlockSpec((1, 6, D_MODEL), lambda i: (mod_row(i, tm), 0, 0)),
                _resident((D_MODEL, D_MODEL), lambda i: (0, 0)),
                _resident((1, D_MODEL), lambda i: (0, 0)),
                _resident((D_MODEL, D_FF), lambda i: (0, 0)),
                _resident((D_MODEL, D_FF), lambda i: (0, 0)),
                _resident((D_FF, D_MODEL), lambda i: (0, 0))]
    args = [x, yp, yf, at, mod, w_out, g2, w1, w3, w2]
    if final_g is not None:
        in_specs.append(_resident((1, D_MODEL), lambda i: (0, 0)))
        args.append(final_g)
    return pl.pallas_call(
        functools.partial(_outffn_kernel, final=final_g is not None),
        grid=(n // tm,),
        in_specs=in_specs,
        out_specs=row(D_MODEL),
        out_shape=jax.ShapeDtypeStruct((n, D_MODEL), F32),
        compiler_params=_params(1),
        name="outproj_ffn",
    )(*args)


def _latent_mod_row(i, tm):
    return (i * tm) // SEQ


def _ctx_mod_row(i, tm):
    return BATCH


def kernel(x, c, ctx, c_ctx, w_ada, b_ada, norm1_g, w_in, pool_w, pool_scale, fourier_w, nat_bias, w_out,
           norm2_g, w_ffn1, w_ffn3, w_ffn2, final_g):
    cond = jnp.concatenate([c, c_ctx[None, :], jnp.zeros((8 - BATCH - 1, D_MODEL), F32)], axis=0)
    mods = _ada_mod(cond, w_ada, b_ada)
    x = x.reshape(BATCH * SEQ, D_MODEL)
    ctx = ctx.reshape(BATCH * CTX_LEN, D_MODEL)
    for l in range(DEPTH):
        last = l == DEPTH - 1
        mod = mods[l]
        g1 = norm1_g[l][None, :]
        g2 = norm2_g[l][None, :]
        w_in_l = w_in[l].astype(BF16)
        pool_w_bd = _group_block_diag(pool_w[l])
        fourier_w_bd = _group_block_diag(fourier_w[l])
        scale = pool_scale[l][None, :]
        ffn_w = (w_out[l].astype(BF16), g2, w_ffn1[l].astype(BF16), w_ffn3[l].astype(BF16),
                 w_ffn2[l].astype(BF16))

        upc, ufc, qkvc = _inproj(ctx, mod, g1, w_in_l, _ctx_mod_row, 256)
        up, uf, qkv = _inproj(x, mod, g1, w_in_l, _latent_mod_row, 1024)
        yp = _pool(up, pool_w_bd, scale, SEQ)
        yf = _fourier_latent(uf, fourier_w_bd)
        at = _neighbourhood_attention(qkv, qkvc, _attention_bias(nat_bias[l]))
        x = _outffn(x, yp, yf, at, mod, *ffn_w, final_g[None, :] if last else None, _latent_mod_row, 512)
        if not last:
            ypc = _pool(upc, pool_w_bd, scale, CTX_LEN)
            yfc = _fourier_ctx(ufc, fourier_w_bd)
            atc = _context_attention(qkvc)
            ctx = _outffn(ctx, ypc, yfc, atc, mod, *ffn_w, None, _ctx_mod_row, 256)
    return x.reshape(BATCH, SEQ, D_MODEL)
```

```python
import functools

import numpy as np
import jax
import jax.numpy as jnp
from jax import lax
from jax.experimental import pallas as pl
from jax.experimental.pallas import tpu as pltpu

D_MODEL = 1024
BATCH = 4
SEQ = 8192
DEPTH = 4
GRID_W = 64
GRID_ROWS = SEQ // GRID_W
CTX_LEN = 256
POOL_WINDOWS = (2, 4, 8, 16)
POOL_WIDTH = 256
FOURIER_WIDTH = 256
GROUP_WIDTH = 64
HEAD_DIM = 64
N_HEADS = 8
N_PAIRS = N_HEADS // 2
ATTN_WIDTH = 512
WIN_R = 8
WIN_C = 16
D_FF = 2816
EPS = 1e-6
NEG_INF = -1e30

FFT_N1 = 128
FFT_N2 = 64
POOL_HALO = 16
N_ROW_CLASSES = 8

V7X_VMEM_LIMIT = 56 * 1024 * 1024

F32 = jnp.float32
BF16 = jnp.bfloat16


def _dot(a, b):
    return jnp.dot(a, b, preferred_element_type=F32)


def _dot_nt(a, b):
    return lax.dot_general(a, b, (((1,), (1,)), ((), ())), preferred_element_type=F32)


def _params(n_grid_dims):
    return pltpu.CompilerParams(dimension_semantics=("arbitrary",) * n_grid_dims,
                                vmem_limit_bytes=V7X_VMEM_LIMIT)


def _resident(block_shape, index_map):
    return pl.BlockSpec(block_shape, index_map, pipeline_mode=pl.Buffered(1))


def _block_diag(m, groups):
    w = m.shape[0]
    out = np.zeros((groups * w, groups * w), m.dtype)
    for g in range(groups):
        out[g * w:(g + 1) * w, g * w:(g + 1) * w] = m
    return out


@functools.lru_cache(maxsize=None)
def _dft_constants():
    two_pi = 2.0 * np.pi
    a2 = two_pi * (np.outer(np.arange(FFT_N2), np.arange(FFT_N2)) % FFT_N2) / FFT_N2
    stage1 = np.concatenate([np.cos(a2), -np.sin(a2)], axis=0)
    k = FFT_N2 * np.arange(FFT_N1)[None, :, None] + np.arange(FFT_N2)[:, None, None]
    n1 = np.arange(FFT_N1)[None, None, :]
    ang = two_pi * ((k * n1) % SEQ) / SEQ
    mr, mi = np.cos(ang), -np.sin(ang)
    stage2 = np.concatenate([np.concatenate([mr, -mi], axis=2),
                             np.concatenate([mi, mr], axis=2)], axis=1)
    ac = two_pi * (np.outer(np.arange(CTX_LEN), np.arange(CTX_LEN)) % CTX_LEN) / CTX_LEN
    ctx_dft = np.concatenate([np.cos(ac), -np.sin(ac)], axis=0)
    ag = two_pi * (np.outer(np.arange(GROUP_WIDTH), np.arange(GROUP_WIDTH)) % GROUP_WIDTH) / GROUP_WIDTH
    chan = np.concatenate([_block_diag(np.cos(ag), 4), _block_diag(np.sin(ag), 4)], axis=0)
    as_bf16 = lambda a: jnp.asarray(a, dtype=F32).astype(BF16)
    return as_bf16(stage1), as_bf16(stage2), as_bf16(ctx_dft), as_bf16(chan)


def _group_block_diag(w):
    out = jnp.zeros((4, GROUP_WIDTH, 4, GROUP_WIDTH), w.dtype)
    for g in range(4):
        out = out.at[g, :, g, :].set(w[g])
    return out.reshape(4 * GROUP_WIDTH, 4 * GROUP_WIDTH).astype(BF16)


def _attention_bias(table):
    col = np.arange(GRID_W)
    cs = np.clip(col - WIN_C // 2, 0, GRID_W - WIN_C)
    col_mask = (col[None, :] >= cs[:, None]) & (col[None, :] < cs[:, None] + WIN_C)
    dc_idx = np.clip(col[None, :] - col[:, None] + WIN_C - 1, 0, 2 * WIN_C - 2)
    rep_rows = np.array([0, 1, 2, 3, 4, GRID_ROWS - 3, GRID_ROWS - 2, GRID_ROWS - 1])
    rs = np.clip(rep_rows - WIN_R // 2, 0, GRID_ROWS - WIN_R)
    dr_base = rs - rep_rows + WIN_R - 1
    onehot = (dc_idx[None] == np.arange(2 * WIN_C - 1)[:, None, None]).astype(np.float32)
    bt = jnp.einsum('hrd,dcj->hrcj', table.astype(F32), jnp.asarray(onehot),
                    precision=lax.Precision.HIGHEST)
    bt = jnp.where(col_mask[None, None], bt, NEG_INF)
    b = jnp.stack([bt[:, int(d):int(d) + WIN_R] for d in dr_base], axis=0)
    b = b.transpose(0, 1, 3, 2, 4)
    return b.reshape(N_ROW_CLASSES, N_PAIRS, 2 * GRID_W, WIN_R * GRID_W)


def _ada_kernel(c_ref, w_ref, b_ref, o_ref):
    c = c_ref[...]
    a = (c * jax.nn.sigmoid(c)).astype(BF16)
    o_ref[0] = _dot(a, w_ref[0].astype(BF16)) + b_ref[0]


def _ada_mod(cond, w_ada, b_ada):
    tn = 1536
    out = pl.pallas_call(
        _ada_kernel,
        grid=(DEPTH, 6 * D_MODEL // tn),
        in_specs=[_resident((8, D_MODEL), lambda l, j: (0, 0)),
                  pl.BlockSpec((1, D_MODEL, tn), lambda l, j: (l, 0, j)),
                  pl.BlockSpec((1, 1, tn), lambda l, j: (l, 0, j))],
        out_specs=pl.BlockSpec((1, 8, tn), lambda l, j: (l, 0, j)),
        out_shape=jax.ShapeDtypeStruct((DEPTH, 8, 6 * D_MODEL), F32),
        compiler_params=_params(2),
        name="ada_mod",
    )(cond, w_ada, b_ada.reshape(DEPTH, 1, 6 * D_MODEL))
    return out.reshape(DEPTH, 8, 6, D_MODEL)


def _rms(x):
    return x * lax.rsqrt(jnp.mean(x * x, axis=-1, keepdims=True) + EPS)


def _inproj_kernel(x_ref, mod_ref, g_ref, w_ref, up_ref, uf_ref, qkv_ref):
    y = _rms(x_ref[...]) * g_ref[...]
    h = (y * (1.0 + mod_ref[0, 1:2, :]) + mod_ref[0, 0:1, :]).astype(BF16)
    up_ref[...] = _dot(h, w_ref[:, 0:256]).astype(BF16)
    uf_ref[...] = _dot(h, w_ref[:, 256:512]).astype(BF16)
    qkv_ref[:, 0:512] = (_dot(h, w_ref[:, 512:1024]) * (HEAD_DIM ** -0.5)).astype(BF16)
    qkv_ref[:, 512:1024] = _dot(h, w_ref[:, 1024:1536]).astype(BF16)
    qkv_ref[:, 1024:1536] = _dot(h, w_ref[:, 1536:2048]).astype(BF16)


def _inproj(x, mod, g, w_in, mod_row, tm):
    n = x.shape[0]
    return pl.pallas_call(
        _inproj_kernel,
        grid=(n // tm,),
        in_specs=[pl.BlockSpec((tm, D_MODEL), lambda i: (i, 0)),
                  pl.BlockSpec((1, 6, D_MODEL), lambda i: (mod_row(i, tm), 0, 0)),
                  _resident((1, D_MODEL), lambda i: (0, 0)),
                  _resident((D_MODEL, 2048), lambda i: (0, 0))],
        out_specs=[pl.BlockSpec((tm, 256), lambda i: (i, 0)),
                   pl.BlockSpec((tm, 256), lambda i: (i, 0)),
                   pl.BlockSpec((tm, 1536), lambda i: (i, 0))],
        out_shape=[jax.ShapeDtypeStruct((n, 256), BF16),
                   jax.ShapeDtypeStruct((n, 256), BF16),
                   jax.ShapeDtypeStruct((n, 1536), BF16)],
        compiler_params=_params(1),
        name="inproj",
    )(x, mod, g, w_in)


POOL_TILE = 256


@functools.lru_cache(maxsize=None)
def _pool_bands():
    t = np.arange(POOL_TILE)[:, None]
    d = np.arange(POOL_TILE + 2 * POOL_HALO)[None, :] - POOL_HALO - t
    bands = [((d >= -(w // 2)) & (d < w - w // 2)) for w in POOL_WINDOWS]
    return jnp.asarray(np.concatenate(bands, axis=0), dtype=F32).astype(BF16)


def _pool_kernel(band_ref, prev_ref, cur_ref, next_ref, pw_ref, ps_ref, o_ref, *, seq_tiles):
    tm = POOL_TILE
    st = pl.program_id(0) % seq_tiles
    lo_b = jnp.where(st == 0, 0, -POOL_HALO)
    hi_b = jnp.where(st == seq_tiles - 1, tm, tm + POOL_HALO)
    cur = cur_ref[...]
    ub = jnp.concatenate([prev_ref[...], cur, next_ref[...]], axis=0)
    p = lax.broadcasted_iota(jnp.int32, ub.shape, 0) - POOL_HALO
    ub = jnp.where((p >= lo_b) & (p < hi_b), ub, jnp.zeros_like(ub))
    sums = _dot(band_ref[...], ub)
    tcol = lax.broadcasted_iota(jnp.int32, (tm, 1), 0)
    group = lax.broadcasted_iota(jnp.int32, (tm, POOL_WIDTH), 1) // GROUP_WIDTH
    mean = jnp.zeros((tm, POOL_WIDTH), F32)
    for g, w in enumerate(POOL_WINDOWS):
        cnt = jnp.minimum(tcol + (w - w // 2), hi_b) - jnp.maximum(tcol - w // 2, lo_b)
        mean = jnp.where(group == g, sums[g * tm:(g + 1) * tm] / cnt.astype(F32), mean)
    diff = (mean - cur.astype(F32)).astype(BF16)
    o_ref[...] = (_dot(diff, pw_ref[...]) * ps_ref[...]).astype(BF16)


def _pool(up, pool_w_bd, pool_scale, seq_len):
    n = up.shape[0]
    tm = POOL_TILE
    hb = tm // POOL_HALO
    n_halo_blocks = n // POOL_HALO
    bands = _pool_bands()
    return pl.pallas_call(
        functools.partial(_pool_kernel, seq_tiles=seq_len // tm),
        grid=(n // tm,),
        in_specs=[_resident(bands.shape, lambda i: (0, 0)),
                  pl.BlockSpec((POOL_HALO, POOL_WIDTH), lambda i: (jnp.maximum(i * hb - 1, 0), 0)),
                  pl.BlockSpec((tm, POOL_WIDTH), lambda i: (i, 0)),
                  pl.BlockSpec((POOL_HALO, POOL_WIDTH),
                               lambda i: (jnp.minimum((i + 1) * hb, n_halo_blocks - 1), 0)),
                  _resident((POOL_WIDTH, POOL_WIDTH), lambda i: (0, 0)),
                  _resident((1, POOL_WIDTH), lambda i: (0, 0))],
        out_specs=pl.BlockSpec((tm, POOL_WIDTH), lambda i: (i, 0)),
        out_shape=jax.ShapeDtypeStruct((n, POOL_WIDTH), BF16),
        compiler_params=_params(1),
        name="pool",
    )(bands, up, up, up, pool_w_bd, pool_scale)


def _fft1_kernel(m_ref, x_ref, y_ref):
    y_ref[0] = _dot(m_ref[...], x_ref[...]).astype(BF16)


def _fourier_epilogue(zr, zi, cs_ref, w_ref, norm):
    zz = jnp.concatenate([zr, zi], axis=1).astype(BF16)
    f = (_dot(zz, cs_ref[...]) * norm).astype(BF16)
    return _dot(f, w_ref[...]).astype(BF16)


def _fft2_kernel(t_ref, y_ref, cs_ref, w_ref, o_ref, *, kb, norm):
    for kk in range(kb):
        yk = jnp.concatenate([y_ref[0, 0, kk], y_ref[0, 1, kk]], axis=0)
        z = _dot(t_ref[kk], yk)
        o_ref[:, kk * FOURIER_WIDTH:(kk + 1) * FOURIER_WIDTH] = _fourier_epilogue(
            z[:FFT_N1], z[FFT_N1:], cs_ref, w_ref, norm)


def _fourier_latent(uf, fourier_w_bd):
    stage1, stage2, _, chan = _dft_constants()
    cols = FFT_N1 * FOURIER_WIDTH
    tc = 8192
    x = uf.reshape(BATCH * FFT_N2, cols)
    y = pl.pallas_call(
        _fft1_kernel,
        grid=(BATCH, cols // tc),
        in_specs=[_resident((2 * FFT_N2, FFT_N2), lambda b, j: (0, 0)),
                  pl.BlockSpec((FFT_N2, tc), lambda b, j: (b, j))],
        out_specs=pl.BlockSpec((1, 2 * FFT_N2, tc), lambda b, j: (b, 0, j)),
        out_shape=jax.ShapeDtypeStruct((BATCH, 2 * FFT_N2, cols), BF16),
        compiler_params=_params(2),
        name="fft_stage1",
    )(stage1, x)
    y = y.reshape(BATCH, 2, FFT_N2, FFT_N1, FOURIER_WIDTH)
    kb = 8
    out = pl.pallas_call(
        functools.partial(_fft2_kernel, kb=kb, norm=float((SEQ * GROUP_WIDTH) ** -0.5)),
        grid=(FFT_N2 // kb, BATCH),
        in_specs=[pl.BlockSpec((kb, 2 * FFT_N1, 2 * FFT_N1), lambda j, b: (j, 0, 0)),
                  pl.BlockSpec((1, 2, kb, FFT_N1, FOURIER_WIDTH), lambda j, b: (b, 0, j, 0, 0)),
                  _resident((2 * FOURIER_WIDTH, FOURIER_WIDTH), lambda j, b: (0, 0)),
                  _resident((FOURIER_WIDTH, FOURIER_WIDTH), lambda j, b: (0, 0))],
        out_specs=pl.BlockSpec((FFT_N1, kb * FOURIER_WIDTH), lambda j, b: (b, j)),
        out_shape=jax.ShapeDtypeStruct((BATCH * FFT_N1, FFT_N2 * FOURIER_WIDTH), BF16),
        compiler_params=_params(2),
        name="fft_stage2",
    )(stage2, y, chan, fourier_w_bd)
    return out.reshape(BATCH * SEQ, FOURIER_WIDTH)


def _fourier_ctx_kernel(m_ref, u_ref, cs_ref, w_ref, o_ref, *, norm):
    z = _dot(m_ref[...], u_ref[...])
    o_ref[...] = _fourier_epilogue(z[:CTX_LEN], z[CTX_LEN:], cs_ref, w_ref, norm)


def _fourier_ctx(ufc, fourier_w_bd):
    _, _, ctx_dft, chan = _dft_constants()
    return pl.pallas_call(
        functools.partial(_fourier_ctx_kernel, norm=float((CTX_LEN * GROUP_WIDTH) ** -0.5)),
        grid=(BATCH,),
        in_specs=[_resident((2 * CTX_LEN, CTX_LEN), lambda b: (0, 0)),
                  pl.BlockSpec((CTX_LEN, FOURIER_WIDTH), lambda b: (b, 0)),
                  _resident((2 * FOURIER_WIDTH, FOURIER_WIDTH), lambda b: (0, 0)),
                  _resident((FOURIER_WIDTH, FOURIER_WIDTH), lambda b: (0, 0))],
        out_specs=pl.BlockSpec((CTX_LEN, FOURIER_WIDTH), lambda b: (b, 0)),
        out_shape=jax.ShapeDtypeStruct((BATCH * CTX_LEN, FOURIER_WIDTH), BF16),
        compiler_params=_params(1),
        name="fourier_ctx",
    )(ctx_dft, ufc, chan, fourier_w_bd)


def _split_pair(x, even):
    zero = jnp.zeros_like(x)
    return jnp.concatenate([jnp.where(even, x, zero), jnp.where(even, zero, x)], axis=0)


def _softmax_pv(scores, values, n_q, even):
    m = functools.reduce(jnp.maximum, [jnp.max(s, axis=1, keepdims=True) for s in scores])
    e = [jnp.exp(s - m) for s in scores]
    l = functools.reduce(jnp.add, [jnp.sum(x, axis=1, keepdims=True) for x in e])
    o = functools.reduce(jnp.add, [_dot(x.astype(BF16), v) for x, v in zip(e, values)])
    o = o / l
    return jnp.where(even, o[:n_q], o[n_q:])


ATTN_TILE_ROWS = 8
ATTN_TILE = ATTN_TILE_ROWS * GRID_W


def _nat_kernel(q_ref, kp_ref, kc_ref, kn_ref, vp_ref, vc_ref, vn_ref, kx_ref, vx_ref, bias_ref,
                o_ref, kbuf, vbuf, sbuf_a, sbuf_b):
    t = pl.program_id(1)
    kbuf[0:ATTN_TILE] = kp_ref[...]
    kbuf[ATTN_TILE:2 * ATTN_TILE] = kc_ref[...]
    kbuf[2 * ATTN_TILE:3 * ATTN_TILE] = kn_ref[...]
    vbuf[0:ATTN_TILE] = vp_ref[...]
    vbuf[ATTN_TILE:2 * ATTN_TILE] = vc_ref[...]
    vbuf[2 * ATTN_TILE:3 * ATTN_TILE] = vn_ref[...]
    even = lax.broadcasted_iota(jnp.int32, (GRID_W, 128), 1) < HEAD_DIM
    n_win = WIN_R * GRID_W

    def window_offset(j):
        r = t * ATTN_TILE_ROWS + j
        rs = jnp.clip(r - WIN_R // 2, 0, GRID_ROWS - WIN_R)
        return pl.multiple_of((rs - (t - 1) * ATTN_TILE_ROWS) * GRID_W, GRID_W)

    def query_offset(j):
        return j * GRID_W if isinstance(j, int) else pl.multiple_of(j * GRID_W, GRID_W)

    def scores(j, sbuf):
        r = t * ATTN_TILE_ROWS + j
        off = window_offset(j)
        cls = jnp.where(r < 4, r, jnp.where(r > GRID_ROWS - 4, r - (GRID_ROWS - N_ROW_CLASSES), 4))
        qoff = query_offset(j)
        for p in range(N_PAIRS):
            lanes = slice(128 * p, 128 * (p + 1))
            qs = _split_pair(q_ref[pl.ds(qoff, GRID_W), lanes], even)
            sbuf[p, :, 0:n_win] = _dot_nt(qs, kbuf[pl.ds(off, n_win), lanes]) + bias_ref[cls, p]
            sbuf[p, :, n_win:] = _dot_nt(qs, kx_ref[:, lanes])

    def outputs(j, sbuf):
        off = window_offset(j)
        qoff = query_offset(j)
        for p in range(N_PAIRS):
            lanes = slice(128 * p, 128 * (p + 1))
            s = sbuf[p]
            e = jnp.exp(s - jnp.max(s, axis=1, keepdims=True))
            l = jnp.sum(e, axis=1, keepdims=True)
            e = e.astype(BF16)
            o = _dot(e[:, :n_win], vbuf[pl.ds(off, n_win), lanes]) + _dot(e[:, n_win:], vx_ref[:, lanes])
            o = o / l
            o_ref[pl.ds(qoff, GRID_W), lanes] = jnp.where(even, o[:GRID_W], o[GRID_W:]).astype(BF16)

    scores(0, sbuf_a)

    def body(i, carry):
        scores(2 * i + 1, sbuf_b)
        outputs(2 * i, sbuf_a)
        scores(2 * i + 2, sbuf_a)
        outputs(2 * i + 1, sbuf_b)
        return carry

    lax.fori_loop(0, ATTN_TILE_ROWS // 2 - 1, body, 0)
    scores(ATTN_TILE_ROWS - 1, sbuf_b)
    outputs(ATTN_TILE_ROWS - 2, sbuf_a)
    outputs(ATTN_TILE_ROWS - 1, sbuf_b)


def _neighbourhood_attention(qkv, qkv_ctx, bias):
    tiles = SEQ // ATTN_TILE
    cur = lambda col: (lambda b, t: (b * tiles + t, col))
    prev = lambda col: (lambda b, t: (b * tiles + jnp.maximum(t - 1, 0), col))
    nxt = lambda col: (lambda b, t: (b * tiles + jnp.minimum(t + 1, tiles - 1), col))
    blk = lambda f: pl.BlockSpec((ATTN_TILE, ATTN_WIDTH), f)
    return pl.pallas_call(
        _nat_kernel,
        grid=(BATCH, tiles),
        in_specs=[blk(cur(0)), blk(prev(1)), blk(cur(1)), blk(nxt(1)),
                  blk(prev(2)), blk(cur(2)), blk(nxt(2)),
                  pl.BlockSpec((CTX_LEN, ATTN_WIDTH), lambda b, t: (b, 1)),
                  pl.BlockSpec((CTX_LEN, ATTN_WIDTH), lambda b, t: (b, 2)),
                  _resident(bias.shape, lambda b, t: (0, 0, 0, 0))],
        out_specs=blk(cur(0)),
        out_shape=jax.ShapeDtypeStruct((BATCH * SEQ, ATTN_WIDTH), BF16),
        scratch_shapes=[pltpu.VMEM((3 * ATTN_TILE, ATTN_WIDTH), BF16),
                        pltpu.VMEM((3 * ATTN_TILE, ATTN_WIDTH), BF16),
                        pltpu.VMEM((N_PAIRS, 2 * GRID_W, WIN_R * GRID_W + CTX_LEN), F32),
                        pltpu.VMEM((N_PAIRS, 2 * GRID_W, WIN_R * GRID_W + CTX_LEN), F32)],
        compiler_params=_params(2),
        name="nat_attention",
    )(qkv, qkv, qkv, qkv, qkv, qkv, qkv, qkv_ctx, qkv_ctx, bias)


def _ctx_attn_kernel(q_ref, k_ref, v_ref, o_ref):
    even = lax.broadcasted_iota(jnp.int32, (CTX_LEN, 128), 1) < HEAD_DIM
    for p in range(N_PAIRS):
        lanes = slice(128 * p, 128 * (p + 1))
        qs = _split_pair(q_ref[:, lanes], even)
        s = _dot_nt(qs, k_ref[:, lanes])
        o_ref[:, lanes] = _softmax_pv([s], [v_ref[:, lanes]], CTX_LEN, even).astype(BF16)


def _context_attention(qkv_ctx):
    blk = lambda col: pl.BlockSpec((CTX_LEN, ATTN_WIDTH), lambda b: (b, col))
    return pl.pallas_call(
        _ctx_attn_kernel,
        grid=(BATCH,),
        in_specs=[blk(0), blk(1), blk(2)],
        out_specs=blk(0),
        out_shape=jax.ShapeDtypeStruct((BATCH * CTX_LEN, ATTN_WIDTH), BF16),
        compiler_params=_params(1),
        name="ctx_attention",
    )(qkv_ctx, qkv_ctx, qkv_ctx)


FFN_CHUNK = D_FF // 2


def _outffn_kernel(x_ref, yp_ref, yf_ref, at_ref, mod_ref, wo_ref, g2_ref, w1_ref, w3_ref, w2_ref,
                   *rest, final):
    o_ref = rest[-1]
    y = jnp.concatenate([yp_ref[...], yf_ref[...], at_ref[...]], axis=1)
    x1 = x_ref[...] + mod_ref[0, 2:3, :] * _dot(y, wo_ref[...])
    h = (_rms(x1) * g2_ref[...] * (1.0 + mod_ref[0, 4:5, :]) + mod_ref[0, 3:4, :]).astype(BF16)
    ffn = jnp.zeros_like(x1)
    for c in range(D_FF // FFN_CHUNK):
        cols = slice(c * FFN_CHUNK, (c + 1) * FFN_CHUNK)
        a = _dot(h, w1_ref[:, cols])
        gated = (a * jax.nn.sigmoid(a) * _dot(h, w3_ref[:, cols])).astype(BF16)
        ffn = ffn + _dot(gated, w2_ref[cols, :])
    x2 = x1 + mod_ref[0, 5:6, :] * ffn
    if final:
        x2 = _rms(x2) * rest[0][...]
    o_ref[...] = x2


def _outffn(x, yp, yf, at, mod, w_out, g2, w1, w3, w2, final_g, mod_row, tm):
    n = x.shape[0]
    row = lambda w: pl.BlockSpec((tm, w), lambda i: (i, 0))
    in_specs = [row(D_MODEL), row(256), row(256), row(512),
                pl.BlockSpec((1, 6, D_MODEL), lambda i: (mod_row(i, tm), 0, 0)),
                _resident((D_MODEL, D_MODEL), lambda i: (0, 0)),
                _resident((1, D_MODEL), lambda i: (0, 0)),
                _resident((D_MODEL, D_FF), lambda i: (0, 0)),
                _resident((D_MODEL, D_FF), lambda i: (0, 0)),
                _resident((D_FF, D_MODEL), lambda i: (0, 0))]
    args = [x, yp, yf, at, mod, w_out, g2, w1, w3, w2]
    if final_g is not None:
        in_specs.append(_resident((1, D_MODEL), lambda i: (0, 0)))
        args.append(final_g)
    return pl.pallas_call(
        functools.partial(_outffn_kernel, final=final_g is not None),
        grid=(n // tm,),
        in_specs=in_specs,
        out_specs=row(D_MODEL),
        out_shape=jax.ShapeDtypeStruct((n, D_MODEL), F32),
        compiler_params=_params(1),
        name="outproj_ffn",
    )(*args)


def _latent_mod_row(i, tm):
    return (i * tm) // SEQ


def _ctx_mod_row(i, tm):
    return BATCH


def kernel(x, c, ctx, c_ctx, w_ada, b_ada, norm1_g, w_in, pool_w, pool_scale, fourier_w, nat_bias, w_out,
           norm2_g, w_ffn1, w_ffn3, w_ffn2, final_g):
    cond = jnp.concatenate([c, c_ctx[None, :], jnp.zeros((8 - BATCH - 1, D_MODEL), F32)], axis=0)
    mods = _ada_mod(cond, w_ada, b_ada)
    x = x.reshape(BATCH * SEQ, D_MODEL)
    ctx = ctx.reshape(BATCH * CTX_LEN, D_MODEL)
    for l in range(DEPTH):
        last = l == DEPTH - 1
        mod = mods[l]
        g1 = norm1_g[l][None, :]
        g2 = norm2_g[l][None, :]
        w_in_l = w_in[l].astype(BF16)
        pool_w_bd = _group_block_diag(pool_w[l])
        fourier_w_bd = _group_block_diag(fourier_w[l])
        scale = pool_scale[l][None, :]
        ffn_w = (w_out[l].astype(BF16), g2, w_ffn1[l].astype(BF16), w_ffn3[l].astype(BF16),
                 w_ffn2[l].astype(BF16))

        upc, ufc, qkvc = _inproj(ctx, mod, g1, w_in_l, _ctx_mod_row, 256)
        up, uf, qkv = _inproj(x, mod, g1, w_in_l, _latent_mod_row, 1024)
        yp = _pool(up, pool_w_bd, scale, SEQ)
        yf = _fourier_latent(uf, fourier_w_bd)
        at = _neighbourhood_attention(qkv, qkvc, _attention_bias(nat_bias[l]))
        x = _outffn(x, yp, yf, at, mod, *ffn_w, final_g[None, :] if last else None, _latent_mod_row, 512)
        if not last:
            ypc = _pool(upc, pool_w_bd, scale, CTX_LEN)
            yfc = _fourier_ctx(ufc, fourier_w_bd)
            atc = _context_attention(qkvc)
            ctx = _outffn(ctx, ypc, yfc, atc, mod, *ffn_w, None, _ctx_mod_row, 256)
    return x.reshape(BATCH, SEQ, D_MODEL)
```

```python
import functools

import numpy as np
import jax
import jax.numpy as jnp
from jax import lax
from jax.experimental import pallas as pl
from jax.experimental.pallas import tpu as pltpu

D_MODEL = 1024
BATCH = 4
SEQ = 8192
DEPTH = 4
GRID_W = 64
GRID_ROWS = SEQ // GRID_W
CTX_LEN = 256
POOL_WINDOWS = (2, 4, 8, 16)
POOL_WIDTH = 256
FOURIER_WIDTH = 256
GROUP_WIDTH = 64
HEAD_DIM = 64
N_HEADS = 8
N_PAIRS = N_HEADS // 2
ATTN_WIDTH = 512
WIN_R = 8
WIN_C = 16
D_FF = 2816
EPS = 1e-6
NEG_INF = -1e30
LOG2_E = 1.4426950408889634
Q_SCALE = HEAD_DIM ** -0.5 * LOG2_E

FFT_N1 = 128
FFT_N2 = 64
POOL_HALO = 16
N_ROW_CLASSES = 8

V7X_VMEM_LIMIT = 56 * 1024 * 1024

F32 = jnp.float32
BF16 = jnp.bfloat16


def _dot(a, b):
    return jnp.dot(a, b, preferred_element_type=F32)


def _dot_nt(a, b):
    return lax.dot_general(a, b, (((1,), (1,)), ((), ())), preferred_element_type=F32)


def _params(n_grid_dims):
    return pltpu.CompilerParams(dimension_semantics=("arbitrary",) * n_grid_dims,
                                vmem_limit_bytes=V7X_VMEM_LIMIT)


def _resident(block_shape, index_map):
    return pl.BlockSpec(block_shape, index_map, pipeline_mode=pl.Buffered(1))


def _block_diag(m, groups):
    w = m.shape[0]
    out = np.zeros((groups * w, groups * w), m.dtype)
    for g in range(groups):
        out[g * w:(g + 1) * w, g * w:(g + 1) * w] = m
    return out


@functools.lru_cache(maxsize=None)
def _dft_constants():
    two_pi = 2.0 * np.pi
    a2 = two_pi * (np.outer(np.arange(FFT_N2), np.arange(FFT_N2)) % FFT_N2) / FFT_N2
    stage1 = np.stack([np.cos(a2), -np.sin(a2)], axis=1).reshape(2 * FFT_N2, FFT_N2)
    k = FFT_N2 * np.arange(FFT_N1)[None, :, None] + np.arange(FFT_N2)[:, None, None]
    n1 = np.arange(FFT_N1)[None, None, :]
    ang = two_pi * ((k * n1) % SEQ) / SEQ
    mr, mi = np.cos(ang), -np.sin(ang)
    stage2 = np.concatenate([np.concatenate([mr, -mi], axis=2),
                             np.concatenate([mi, mr], axis=2)], axis=1)
    ac = two_pi * (np.outer(np.arange(CTX_LEN), np.arange(CTX_LEN)) % CTX_LEN) / CTX_LEN
    ctx_dft = np.concatenate([np.cos(ac), -np.sin(ac)], axis=0)
    ag = two_pi * (np.outer(np.arange(GROUP_WIDTH), np.arange(GROUP_WIDTH)) % GROUP_WIDTH) / GROUP_WIDTH
    chan = np.concatenate([_block_diag(np.cos(ag), 4), _block_diag(np.sin(ag), 4)], axis=0)
    as_bf16 = lambda a: jnp.asarray(a, dtype=F32).astype(BF16)
    return as_bf16(stage1), as_bf16(stage2), as_bf16(ctx_dft), as_bf16(chan)


def _group_block_diag(w):
    out = jnp.zeros((4, GROUP_WIDTH, 4, GROUP_WIDTH), w.dtype)
    for g in range(4):
        out = out.at[g, :, g, :].set(w[g])
    return out.reshape(4 * GROUP_WIDTH, 4 * GROUP_WIDTH).astype(BF16)


def _attention_bias(table):
    col = np.arange(GRID_W)
    cs = np.clip(col - WIN_C // 2, 0, GRID_W - WIN_C)
    col_mask = (col[None, :] >= cs[:, None]) & (col[None, :] < cs[:, None] + WIN_C)
    dc_idx = np.clip(col[None, :] - col[:, None] + WIN_C - 1, 0, 2 * WIN_C - 2)
    rep_rows = np.array([0, 1, 2, 3, 4, GRID_ROWS - 3, GRID_ROWS - 2, GRID_ROWS - 1])
    rs = np.clip(rep_rows - WIN_R // 2, 0, GRID_ROWS - WIN_R)
    dr_base = rs - rep_rows + WIN_R - 1
    onehot = (dc_idx[None] == np.arange(2 * WIN_C - 1)[:, None, None]).astype(np.float32)
    bt = jnp.einsum('hrd,dcj->hrcj', table.astype(F32), jnp.asarray(onehot),
                    precision=lax.Precision.HIGHEST)
    bt = jnp.where(col_mask[None, None], bt * LOG2_E, NEG_INF)
    b = jnp.stack([bt[:, int(d):int(d) + WIN_R] for d in dr_base], axis=0)
    b = b.transpose(0, 1, 3, 2, 4)
    return b.reshape(N_ROW_CLASSES, N_PAIRS, 2 * GRID_W, WIN_R * GRID_W)


def _ada_kernel(c_ref, w_ref, b_ref, o_ref):
    c = c_ref[...]
    a = (c * jax.nn.sigmoid(c)).astype(BF16)
    o_ref[0] = _dot(a, w_ref[0].astype(BF16)) + b_ref[0]


def _ada_mod(cond, w_ada, b_ada):
    tn = 1536
    out = pl.pallas_call(
        _ada_kernel,
        grid=(DEPTH, 6 * D_MODEL // tn),
        in_specs=[_resident((8, D_MODEL), lambda l, j: (0, 0)),
                  pl.BlockSpec((1, D_MODEL, tn), lambda l, j: (l, 0, j)),
                  pl.BlockSpec((1, 1, tn), lambda l, j: (l, 0, j))],
        out_specs=pl.BlockSpec((1, 8, tn), lambda l, j: (l, 0, j)),
        out_shape=jax.ShapeDtypeStruct((DEPTH, 8, 6 * D_MODEL), F32),
        compiler_params=_params(2),
        name="ada_mod",
    )(cond, w_ada, b_ada.reshape(DEPTH, 1, 6 * D_MODEL))
    return out.reshape(DEPTH, 8, 6, D_MODEL)


def _rms(x):
    return x * lax.rsqrt(jnp.mean(x * x, axis=-1, keepdims=True) + EPS)


def _inproj_kernel(x_ref, mod_ref, g_ref, w_ref, up_ref, uf_ref, qkv_ref):
    y = _rms(x_ref[...]) * g_ref[...]
    h = (y * (1.0 + mod_ref[0, 1:2, :]) + mod_ref[0, 0:1, :]).astype(BF16)
    up_ref[...] = _dot(h, w_ref[:, 0:256]).astype(BF16)
    uf_ref[...] = _dot(h, w_ref[:, 256:512]).astype(BF16)
    qkv_ref[:, 0:512] = (_dot(h, w_ref[:, 512:1024]) * Q_SCALE).astype(BF16)
    qkv_ref[:, 512:1024] = _dot(h, w_ref[:, 1024:1536]).astype(BF16)
    qkv_ref[:, 1024:1536] = _dot(h, w_ref[:, 1536:2048]).astype(BF16)


def _inproj(x, mod, g, w_in, mod_row, tm):
    n = x.shape[0]
    return pl.pallas_call(
        _inproj_kernel,
        grid=(n // tm,),
        in_specs=[pl.BlockSpec((tm, D_MODEL), lambda i: (i, 0)),
                  pl.BlockSpec((1, 6, D_MODEL), lambda i: (mod_row(i, tm), 0, 0)),
                  _resident((1, D_MODEL), lambda i: (0, 0)),
                  _resident((D_MODEL, 2048), lambda i: (0, 0))],
        out_specs=[pl.BlockSpec((tm, 256), lambda i: (i, 0)),
                   pl.BlockSpec((tm, 256), lambda i: (i, 0)),
                   pl.BlockSpec((tm, 1536), lambda i: (i, 0))],
        out_shape=[jax.ShapeDtypeStruct((n, 256), BF16),
                   jax.ShapeDtypeStruct((n, 256), BF16),
                   jax.ShapeDtypeStruct((n, 1536), BF16)],
        compiler_params=_params(1),
        name="inproj",
    )(x, mod, g, w_in)


POOL_SUB = 256
POOL_BAND_K = POOL_SUB + 2 * POOL_HALO


@functools.lru_cache(maxsize=None)
def _pool_bands():
    t = np.arange(POOL_SUB)[:, None]
    d = np.arange(POOL_BAND_K)[None, :] - POOL_HALO - t
    member = [((d >= -(w // 2)) & (d < w - w // 2)).astype(np.float64) for w in POOL_WINDOWS]
    centred = [m / w - (d == 0) for m, w in zip(member, POOL_WINDOWS)]
    interior = np.stack([np.concatenate(centred[0:2], axis=1), np.concatenate(centred[2:4], axis=1)], axis=0)
    as_bf16 = lambda a: jnp.asarray(a, dtype=F32).astype(BF16)
    return as_bf16(np.concatenate(member, axis=0)), as_bf16(interior)


def _pool_kernel(band_ref, inner_ref, prev_ref, cur_ref, next_ref, pw_ref, ps_ref, o_ref, *, seq_tiles, nsub):
    st = pl.program_id(0) % seq_tiles
    ub_all = jnp.concatenate([prev_ref[...], cur_ref[...], next_ref[...]], axis=0)
    left = lax.broadcasted_iota(jnp.int32, (POOL_BAND_K, 128), 1) < GROUP_WIDTH

    for s in range(nsub):
        ub = ub_all[s * POOL_SUB:s * POOL_SUB + POOL_BAND_K]
        cur = ub[POOL_HALO:POOL_HALO + POOL_SUB]

        def interior(ub=ub):
            halves = []
            for h in range(2):
                t = ub[:, 128 * h:128 * (h + 1)]
                zero = jnp.zeros_like(t)
                stacked = jnp.concatenate([jnp.where(left, t, zero), jnp.where(left, zero, t)], axis=0)
                halves.append(_dot(inner_ref[h], stacked))
            return jnp.concatenate(halves, axis=1).astype(BF16)

        def clipped(ub=ub, cur=cur, s=s):
            lo_b = jnp.where(st == 0, 0, -POOL_HALO) if s == 0 else -POOL_HALO
            hi_b = (jnp.where(st == seq_tiles - 1, POOL_SUB, POOL_SUB + POOL_HALO) if s == nsub - 1
                    else POOL_SUB + POOL_HALO)
            p = lax.broadcasted_iota(jnp.int32, ub.shape, 0) - POOL_HALO
            ubv = jnp.where((p >= lo_b) & (p < hi_b), ub, jnp.zeros_like(ub))
            sums = _dot(band_ref[...], ubv)
            tcol = lax.broadcasted_iota(jnp.int32, (POOL_SUB, 1), 0)
            group = lax.broadcasted_iota(jnp.int32, (POOL_SUB, POOL_WIDTH), 1) // GROUP_WIDTH
            mean = jnp.zeros((POOL_SUB, POOL_WIDTH), F32)
            for g, w in enumerate(POOL_WINDOWS):
                cnt = jnp.minimum(tcol + (w - w // 2), hi_b) - jnp.maximum(tcol - w // 2, lo_b)
                mean = jnp.where(group == g, sums[g * POOL_SUB:(g + 1) * POOL_SUB] / cnt.astype(F32), mean)
            return (mean - cur.astype(F32)).astype(BF16)

        at_start = (st == 0) if s == 0 else None
        at_end = (st == seq_tiles - 1) if s == nsub - 1 else None
        if at_start is None and at_end is None:
            diff = interior()
        else:
            edge = at_start if at_end is None else (at_end if at_start is None else at_start | at_end)
            diff = lax.cond(edge, clipped, interior)
        o_ref[s * POOL_SUB:(s + 1) * POOL_SUB] = (_dot(diff, pw_ref[...]) * ps_ref[...]).astype(BF16)


def _pool(up, pool_w_bd, pool_scale, seq_len, tm):
    n = up.shape[0]
    hb = tm // POOL_HALO
    n_halo_blocks = n // POOL_HALO
    bands, inner = _pool_bands()
    return pl.pallas_call(
        functools.partial(_pool_kernel, seq_tiles=seq_len // tm, nsub=tm // POOL_SUB),
        grid=(n // tm,),
        in_specs=[_resident(bands.shape, lambda i: (0, 0)),
                  _resident(inner.shape, lambda i: (0, 0, 0)),
                  pl.BlockSpec((POOL_HALO, POOL_WIDTH), lambda i: (jnp.maximum(i * hb - 1, 0), 0)),
                  pl.BlockSpec((tm, POOL_WIDTH), lambda i: (i, 0)),
                  pl.BlockSpec((POOL_HALO, POOL_WIDTH),
                               lambda i: (jnp.minimum((i + 1) * hb, n_halo_blocks - 1), 0)),
                  _resident((POOL_WIDTH, POOL_WIDTH), lambda i: (0, 0)),
                  _resident((1, POOL_WIDTH), lambda i: (0, 0))],
        out_specs=pl.BlockSpec((tm, POOL_WIDTH), lambda i: (i, 0)),
        out_shape=jax.ShapeDtypeStruct((n, POOL_WIDTH), BF16),
        compiler_params=_params(1),
        name="pool",
    )(bands, inner, up, up, up, pool_w_bd, pool_scale)


FFT_CHUNK = 16


def _gather_rows(src, start, count):
    return jnp.concatenate([src[h, pl.ds(start, count, stride=FFT_CHUNK), :] for h in range(2)], axis=1)


def _scatter_rows(dst, start, value):
    for h in range(2):
        dst[h, pl.ds(start, value.shape[0], stride=FFT_CHUNK), :] = value[:, 128 * h:128 * (h + 1)]


def _fft1_kernel(m_ref, x_ref, y_ref, xs, ys):
    xf = x_ref[...].astype(F32).reshape(FFT_N2 * FFT_CHUNK, FOURIER_WIDTH)
    xs[0] = xf[:, :128]
    xs[1] = xf[:, 128:]
    for n1 in range(FFT_CHUNK):
        y = _dot(m_ref[...], _gather_rows(xs, n1, FFT_N2).astype(BF16))
        _scatter_rows(ys, n1, y)
    y_ref[...] = jnp.concatenate([ys[0], ys[1]], axis=1).astype(BF16).reshape(y_ref.shape)


def _fourier_epilogue(zz, cs_ref, w_ref, norm):
    f = (_dot(zz, cs_ref[...]) * norm).astype(BF16)
    return _dot(f, w_ref[...])


def _fft2_kernel(t_ref, y_ref, cs_ref, w_ref, o_ref, zz, os, *, norm):
    for kk in range(FFT_CHUNK):
        z = _dot(t_ref[kk], y_ref[kk].reshape(2 * FFT_N1, FOURIER_WIDTH))
        zz[kk * FFT_N1:(kk + 1) * FFT_N1, :] = jnp.concatenate([z[:FFT_N1], z[FFT_N1:]], axis=1).astype(BF16)
    o = _fourier_epilogue(zz[...], cs_ref, w_ref, norm)
    for kk in range(FFT_CHUNK):
        _scatter_rows(os, kk, o[kk * FFT_N1:(kk + 1) * FFT_N1])
    o_ref[...] = jnp.concatenate([os[0], os[1]], axis=1).astype(BF16).reshape(o_ref.shape)


def _fourier_latent(uf, fourier_w_bd):
    stage1, stage2, _, chan = _dft_constants()
    chunks1 = FFT_N1 // FFT_CHUNK
    y = pl.pallas_call(
        _fft1_kernel,
        grid=(BATCH, chunks1),
        in_specs=[_resident((2 * FFT_N2, FFT_N2), lambda b, j: (0, 0)),
                  pl.BlockSpec((FFT_N2, FFT_CHUNK, FOURIER_WIDTH), lambda b, j: (b, j, 0))],
        out_specs=pl.BlockSpec((FFT_N2, 2, FFT_CHUNK, FOURIER_WIDTH), lambda b, j: (b, 0, j, 0)),
        out_shape=jax.ShapeDtypeStruct((BATCH * FFT_N2, 2, FFT_N1, FOURIER_WIDTH), BF16),
        scratch_shapes=[pltpu.VMEM((2, FFT_N2 * FFT_CHUNK, 128), F32),
                        pltpu.VMEM((2, 2 * FFT_N2 * FFT_CHUNK, 128), F32)],
        compiler_params=_params(2),
        name="fft_stage1",
    )(stage1, uf.reshape(BATCH * FFT_N2, FFT_N1, FOURIER_WIDTH))
    chunks2 = FFT_N2 // FFT_CHUNK
    out = pl.pallas_call(
        functools.partial(_fft2_kernel, norm=float((SEQ * GROUP_WIDTH) ** -0.5)),
        grid=(chunks2, BATCH),
        in_specs=[pl.BlockSpec((FFT_CHUNK, 2 * FFT_N1, 2 * FFT_N1), lambda j, b: (j, 0, 0)),
                  pl.BlockSpec((FFT_CHUNK, 2, FFT_N1, FOURIER_WIDTH), lambda j, b: (b * chunks2 + j, 0, 0, 0)),
                  _resident((2 * FOURIER_WIDTH, FOURIER_WIDTH), lambda j, b: (0, 0)),
                  _resident((FOURIER_WIDTH, FOURIER_WIDTH), lambda j, b: (0, 0))],
        out_specs=pl.BlockSpec((FFT_N1, FFT_CHUNK, FOURIER_WIDTH), lambda j, b: (b, j, 0)),
        out_shape=jax.ShapeDtypeStruct((BATCH * FFT_N1, FFT_N2, FOURIER_WIDTH), BF16),
        scratch_shapes=[pltpu.VMEM((FFT_CHUNK * FFT_N1, 2 * FOURIER_WIDTH), BF16),
                        pltpu.VMEM((2, FFT_CHUNK * FFT_N1, 128), F32)],
        compiler_params=_params(2),
        name="fft_stage2",
    )(stage2, y, chan, fourier_w_bd)
    return out.reshape(BATCH * SEQ, FOURIER_WIDTH)


def _fourier_ctx_kernel(m_ref, u_ref, cs_ref, w_ref, o_ref, *, norm):
    z = _dot(m_ref[...], u_ref[...])
    zz = jnp.concatenate([z[:CTX_LEN], z[CTX_LEN:]], axis=1).astype(BF16)
    o_ref[...] = _fourier_epilogue(zz, cs_ref, w_ref, norm).astype(BF16)


def _fourier_ctx(ufc, fourier_w_bd):
    _, _, ctx_dft, chan = _dft_constants()
    return pl.pallas_call(
        functools.partial(_fourier_ctx_kernel, norm=float((CTX_LEN * GROUP_WIDTH) ** -0.5)),
        grid=(BATCH,),
        in_specs=[_resident((2 * CTX_LEN, CTX_LEN), lambda b: (0, 0)),
                  pl.BlockSpec((CTX_LEN, FOURIER_WIDTH), lambda b: (b, 0)),
                  _resident((2 * FOURIER_WIDTH, FOURIER_WIDTH), lambda b: (0, 0)),
                  _resident((FOURIER_WIDTH, FOURIER_WIDTH), lambda b: (0, 0))],
        out_specs=pl.BlockSpec((CTX_LEN, FOURIER_WIDTH), lambda b: (b, 0)),
        out_shape=jax.ShapeDtypeStruct((BATCH * CTX_LEN, FOURIER_WIDTH), BF16),
        compiler_params=_params(1),
        name="fourier_ctx",
    )(ctx_dft, ufc, chan, fourier_w_bd)


def _split_pair(x, even):
    zero = jnp.zeros_like(x)
    return jnp.concatenate([jnp.where(even, x, zero), jnp.where(even, zero, x)], axis=0)


def _softmax_pv(scores, values, n_q, even):
    m = functools.reduce(jnp.maximum, [jnp.max(s, axis=1, keepdims=True) for s in scores])
    e = [jnp.exp2(s - m) for s in scores]
    l = functools.reduce(jnp.add, [jnp.sum(x, axis=1, keepdims=True) for x in e])
    o = functools.reduce(jnp.add, [_dot(x.astype(BF16), v) for x, v in zip(e, values)])
    o = o / l
    return jnp.where(even, o[:n_q], o[n_q:])


ATTN_TILE_ROWS = 16
ATTN_TILE = ATTN_TILE_ROWS * GRID_W
ATTN_HALO = WIN_R * GRID_W
SOFTMAX_ROWS = 16


def _nat_kernel(q_ref, kp_ref, kc_ref, kn_ref, vp_ref, vc_ref, vn_ref, kx_ref, vx_ref, bias_ref,
                o_ref, kbuf, vbuf, sbuf_a, sbuf_b, pbuf):
    t = pl.program_id(1)
    kbuf[0:ATTN_HALO] = kp_ref[...]
    kbuf[ATTN_HALO:ATTN_HALO + ATTN_TILE] = kc_ref[...]
    kbuf[ATTN_HALO + ATTN_TILE:] = kn_ref[...]
    vbuf[0:ATTN_HALO] = vp_ref[...]
    vbuf[ATTN_HALO:ATTN_HALO + ATTN_TILE] = vc_ref[...]
    vbuf[ATTN_HALO + ATTN_TILE:] = vn_ref[...]
    even = lax.broadcasted_iota(jnp.int32, (GRID_W, 128), 1) < HEAD_DIM
    n_win = WIN_R * GRID_W

    def window_offset(j):
        r = t * ATTN_TILE_ROWS + j
        rs = jnp.clip(r - WIN_R // 2, 0, GRID_ROWS - WIN_R)
        return pl.multiple_of((rs - t * ATTN_TILE_ROWS + WIN_R) * GRID_W, GRID_W)

    def query_offset(j):
        return j * GRID_W if isinstance(j, int) else pl.multiple_of(j * GRID_W, GRID_W)

    def scores(j, sbuf):
        r = t * ATTN_TILE_ROWS + j
        off = window_offset(j)
        cls = jnp.where(r < 4, r, jnp.where(r > GRID_ROWS - 4, r - (GRID_ROWS - N_ROW_CLASSES), 4))
        qoff = query_offset(j)
        for p in range(N_PAIRS):
            lanes = slice(128 * p, 128 * (p + 1))
            qs = _split_pair(q_ref[pl.ds(qoff, GRID_W), lanes], even)
            sbuf[p, :, 0:n_win] = _dot_nt(qs, kbuf[pl.ds(off, n_win), lanes]) + bias_ref[cls, p]
            sbuf[p, :, n_win:] = _dot_nt(qs, kx_ref[:, lanes])

    def outputs(j, sbuf):
        off = window_offset(j)
        qoff = query_offset(j)
        for p in range(N_PAIRS):
            lanes = slice(128 * p, 128 * (p + 1))
            sums = []
            for g in range(2 * GRID_W // SOFTMAX_ROWS):
                rows = slice(g * SOFTMAX_ROWS, (g + 1) * SOFTMAX_ROWS)
                s = sbuf[p, rows, :]
                e = jnp.exp2(s - jnp.max(s, axis=1, keepdims=True))
                sums.append(jnp.sum(e, axis=1, keepdims=True))
                pbuf[p, rows, :] = e.astype(BF16)
            l = jnp.concatenate(sums, axis=0)
            o = (_dot(pbuf[p, :, :n_win], vbuf[pl.ds(off, n_win), lanes])
                 + _dot(pbuf[p, :, n_win:], vx_ref[:, lanes]))
            o = o / l
            o_ref[pl.ds(qoff, GRID_W), lanes] = jnp.where(even, o[:GRID_W], o[GRID_W:]).astype(BF16)

    scores(0, sbuf_a)

    def body(i, carry):
        scores(2 * i + 1, sbuf_b)
        outputs(2 * i, sbuf_a)
        scores(jnp.minimum(2 * i + 2, ATTN_TILE_ROWS - 1), sbuf_a)
        outputs(2 * i + 1, sbuf_b)
        return carry

    lax.fori_loop(0, ATTN_TILE_ROWS // 2, body, 0)


def _neighbourhood_attention(qkv, qkv_ctx, bias):
    tiles = SEQ // ATTN_TILE
    halos = SEQ // ATTN_HALO
    per_tile = ATTN_TILE // ATTN_HALO
    cur = lambda col: (lambda b, t: (b * tiles + t, col))
    prev = lambda col: (lambda b, t: (b * halos + jnp.maximum(t * per_tile - 1, 0), col))
    nxt = lambda col: (lambda b, t: (b * halos + jnp.minimum((t + 1) * per_tile, halos - 1), col))
    blk = lambda f: pl.BlockSpec((ATTN_TILE, ATTN_WIDTH), f)
    halo = lambda f: pl.BlockSpec((ATTN_HALO, ATTN_WIDTH), f)
    return pl.pallas_call(
        _nat_kernel,
        grid=(BATCH, tiles),
        in_specs=[blk(cur(0)), halo(prev(1)), blk(cur(1)), halo(nxt(1)),
                  halo(prev(2)), blk(cur(2)), halo(nxt(2)),
                  pl.BlockSpec((CTX_LEN, ATTN_WIDTH), lambda b, t: (b, 1)),
                  pl.BlockSpec((CTX_LEN, ATTN_WIDTH), lambda b, t: (b, 2)),
                  _resident(bias.shape, lambda b, t: (0, 0, 0, 0))],
        out_specs=blk(cur(0)),
        out_shape=jax.ShapeDtypeStruct((BATCH * SEQ, ATTN_WIDTH), BF16),
        scratch_shapes=[pltpu.VMEM((ATTN_TILE + 2 * ATTN_HALO, ATTN_WIDTH), BF16),
                        pltpu.VMEM((ATTN_TILE + 2 * ATTN_HALO, ATTN_WIDTH), BF16),
                        pltpu.VMEM((N_PAIRS, 2 * GRID_W, WIN_R * GRID_W + CTX_LEN), F32),
                        pltpu.VMEM((N_PAIRS, 2 * GRID_W, WIN_R * GRID_W + CTX_LEN), F32),
                        pltpu.VMEM((N_PAIRS, 2 * GRID_W, WIN_R * GRID_W + CTX_LEN), BF16)],
        compiler_params=_params(2),
        name="nat_attention",
    )(qkv, qkv, qkv, qkv, qkv, qkv, qkv, qkv_ctx, qkv_ctx, bias)


def _ctx_attn_kernel(q_ref, k_ref, v_ref, o_ref):
    even = lax.broadcasted_iota(jnp.int32, (CTX_LEN, 128), 1) < HEAD_DIM
    for p in range(N_PAIRS):
        lanes = slice(128 * p, 128 * (p + 1))
        qs = _split_pair(q_ref[:, lanes], even)
        s = _dot_nt(qs, k_ref[:, lanes])
        o_ref[:, lanes] = _softmax_pv([s], [v_ref[:, lanes]], CTX_LEN, even).astype(BF16)


def _context_attention(qkv_ctx):
    blk = lambda col: pl.BlockSpec((CTX_LEN, ATTN_WIDTH), lambda b: (b, col))
    return pl.pallas_call(
        _ctx_attn_kernel,
        grid=(BATCH,),
        in_specs=[blk(0), blk(1), blk(2)],
        out_specs=blk(0),
        out_shape=jax.ShapeDtypeStruct((BATCH * CTX_LEN, ATTN_WIDTH), BF16),
        compiler_params=_params(1),
        name="ctx_attention",
    )(qkv_ctx, qkv_ctx, qkv_ctx)


FFN_CHUNK = D_FF // 2


def _outffn_kernel(x_ref, yp_ref, yf_ref, at_ref, mod_ref, wo_ref, g2_ref, w1_ref, w3_ref, w2_ref,
                   *rest, final):
    o_ref = rest[-1]
    y = jnp.concatenate([yp_ref[...], yf_ref[...], at_ref[...]], axis=1)
    x1 = x_ref[...] + mod_ref[0, 2:3, :] * _dot(y, wo_ref[...])
    h = (_rms(x1) * g2_ref[...] * (1.0 + mod_ref[0, 4:5, :]) + mod_ref[0, 3:4, :]).astype(BF16)
    ffn = jnp.zeros_like(x1)
    for c in range(D_FF // FFN_CHUNK):
        cols = slice(c * FFN_CHUNK, (c + 1) * FFN_CHUNK)
        a = _dot(h, w1_ref[:, cols])
        gated = (a * jax.nn.sigmoid(a) * _dot(h, w3_ref[:, cols])).astype(BF16)
        ffn = ffn + _dot(gated, w2_ref[cols, :])
    x2 = x1 + mod_ref[0, 5:6, :] * ffn
    if final:
        x2 = _rms(x2) * rest[0][...]
    o_ref[...] = x2


def _outffn(x, yp, yf, at, mod, w_out, g2, w1, w3, w2, final_g, mod_row, tm):
    n = x.shape[0]
    row = lambda w: pl.BlockSpec((tm, w), lambda i: (i, 0))
    in_specs = [row(D_MODEL), row(256), row(256), row(512),
                pl.BlockSpec((1, 6, D_MODEL), lambda i: (mod_row(i, tm), 0, 0)),
                _resident((D_MODEL, D_MODEL), lambda i: (0, 0)),
                _resident((1, D_MODEL), lambda i: (0, 0)),
                _resident((D_MODEL, D_FF), lambda i: (0, 0)),
                _resident((D_MODEL, D_FF), lambda i: (0, 0)),
                _resident((D_FF, D_MODEL), lambda i: (0, 0))]
    args = [x, yp, yf, at, mod, w_out, g2, w1, w3, w2]
    if final_g is not None:
        in_specs.append(_resident((1, D_MODEL), lambda i: (0, 0)))
        args.append(final_g)
    return pl.pallas_call(
        functools.partial(_outffn_kernel, final=final_g is not None),
        grid=(n // tm,),
        in_specs=in_specs,
        out_specs=row(D_MODEL),
        out_shape=jax.ShapeDtypeStruct((n, D_MODEL), F32),
        compiler_params=_params(1),
        name="outproj_ffn",
    )(*args)


def _latent_mod_row(i, tm):
    return (i * tm) // SEQ


def _ctx_mod_row(i, tm):
    return BATCH


def kernel(x, c, ctx, c_ctx, w_ada, b_ada, norm1_g, w_in, pool_w, pool_scale, fourier_w, nat_bias, w_out,
           norm2_g, w_ffn1, w_ffn3, w_ffn2, final_g):
    cond = jnp.concatenate([c, c_ctx[None, :], jnp.zeros((8 - BATCH - 1, D_MODEL), F32)], axis=0)
    mods = _ada_mod(cond, w_ada, b_ada)
    x = x.reshape(BATCH * SEQ, D_MODEL)
    ctx = ctx.reshape(BATCH * CTX_LEN, D_MODEL)
    for l in range(DEPTH):
        last = l == DEPTH - 1
        mod = mods[l]
        g1 = norm1_g[l][None, :]
        g2 = norm2_g[l][None, :]
        w_in_l = w_in[l].astype(BF16)
        pool_w_bd = _group_block_diag(pool_w[l])
        fourier_w_bd = _group_block_diag(fourier_w[l])
        scale = pool_scale[l][None, :]
        ffn_w = (w_out[l].astype(BF16), g2, w_ffn1[l].astype(BF16), w_ffn3[l].astype(BF16),
                 w_ffn2[l].astype(BF16))

        upc, ufc, qkvc = _inproj(ctx, mod, g1, w_in_l, _ctx_mod_row, 256)
        up, uf, qkv = _inproj(x, mod, g1, w_in_l, _latent_mod_row, 1024)
        yp = _pool(up, pool_w_bd, scale, SEQ, 1024)
        yf = _fourier_latent(uf, fourier_w_bd)
        at = _neighbourhood_attention(qkv, qkvc, _attention_bias(nat_bias[l]))
        x = _outffn(x, yp, yf, at, mod, *ffn_w, final_g[None, :] if last else None, _latent_mod_row, 512)
        if not last:
            ypc = _pool(upc, pool_w_bd, scale, CTX_LEN, CTX_LEN)
            yfc = _fourier_ctx(ufc, fourier_w_bd)
            atc = _context_attention(qkvc)
            ctx = _outffn(ctx, ypc, yfc, atc, mod, *ffn_w, None, _ctx_mod_row, 256)
    return x.reshape(BATCH, SEQ, D_MODEL)
```

```python
import functools

import numpy as np
import jax
import jax.numpy as jnp
from jax import lax
from jax.experimental import pallas as pl
from jax.experimental.pallas import tpu as pltpu

D_MODEL = 1024
BATCH = 4
SEQ = 8192
DEPTH = 4
GRID_W = 64
GRID_ROWS = SEQ // GRID_W
CTX_LEN = 256
POOL_WINDOWS = (2, 4, 8, 16)
POOL_WIDTH = 256
FOURIER_WIDTH = 256
GROUP_WIDTH = 64
HEAD_DIM = 64
N_HEADS = 8
N_PAIRS = N_HEADS // 2
ATTN_WIDTH = 512
WIN_R = 8
WIN_C = 16
D_FF = 2816
EPS = 1e-6
NEG_INF = -1e30
LOG2_E = 1.4426950408889634
Q_SCALE = HEAD_DIM ** -0.5 * LOG2_E

FFT_N1 = 128
FFT_N2 = 64
POOL_HALO = 16
BIAS_SHAPE = (N_PAIRS, 2 * WIN_R - 2, 2 * GRID_W, 2 * GRID_W)

V7X_VMEM_LIMIT = 56 * 1024 * 1024

F32 = jnp.float32
BF16 = jnp.bfloat16


def _dot(a, b):
    return jnp.dot(a, b, preferred_element_type=F32)


def _dot_nt(a, b):
    return lax.dot_general(a, b, (((1,), (1,)), ((), ())), preferred_element_type=F32)


def _params(n_grid_dims):
    return pltpu.CompilerParams(dimension_semantics=("arbitrary",) * n_grid_dims,
                                vmem_limit_bytes=V7X_VMEM_LIMIT)


def _resident(block_shape, index_map):
    return pl.BlockSpec(block_shape, index_map, pipeline_mode=pl.Buffered(1))


def _block_diag(m, groups):
    w = m.shape[0]
    out = np.zeros((groups * w, groups * w), m.dtype)
    for g in range(groups):
        out[g * w:(g + 1) * w, g * w:(g + 1) * w] = m
    return out


@functools.lru_cache(maxsize=None)
def _dft_constants():
    two_pi = 2.0 * np.pi
    a2 = two_pi * (np.outer(np.arange(FFT_N2), np.arange(FFT_N2)) % FFT_N2) / FFT_N2
    stage1 = np.stack([np.cos(a2), -np.sin(a2)], axis=1).reshape(2 * FFT_N2, FFT_N2)
    k = FFT_N2 * np.arange(FFT_N1)[None, :, None] + np.arange(FFT_N2)[:, None, None]
    n1 = np.arange(FFT_N1)[None, None, :]
    ang = two_pi * ((k * n1) % SEQ) / SEQ
    mr, mi = np.cos(ang), -np.sin(ang)
    stage2 = np.concatenate([np.concatenate([mr, -mi], axis=2),
                             np.concatenate([mi, mr], axis=2)], axis=1)
    ac = two_pi * (np.outer(np.arange(CTX_LEN), np.arange(CTX_LEN)) % CTX_LEN) / CTX_LEN
    ctx_dft = np.concatenate([np.cos(ac), -np.sin(ac)], axis=0)
    ag = two_pi * (np.outer(np.arange(GROUP_WIDTH), np.arange(GROUP_WIDTH)) % GROUP_WIDTH) / GROUP_WIDTH
    chan = np.concatenate([_block_diag(np.cos(ag), 4), _block_diag(np.sin(ag), 4)], axis=0)
    as_bf16 = lambda a: jnp.asarray(a, dtype=F32).astype(BF16)
    return as_bf16(stage1), as_bf16(stage2), as_bf16(ctx_dft), as_bf16(chan)


def _group_block_diag(w):
    out = jnp.zeros((4, GROUP_WIDTH, 4, GROUP_WIDTH), w.dtype)
    for g in range(4):
        out = out.at[g, :, g, :].set(w[g])
    return out.reshape(4 * GROUP_WIDTH, 4 * GROUP_WIDTH).astype(BF16)


def _attention_bias(table):
    col = np.arange(GRID_W)
    cs = np.clip(col - WIN_C // 2, 0, GRID_W - WIN_C)
    col_mask = (col[None, :] >= cs[:, None]) & (col[None, :] < cs[:, None] + WIN_C)
    dc_idx = np.clip(col[None, :] - col[:, None] + WIN_C - 1, 0, 2 * WIN_C - 2)
    onehot = (dc_idx[None] == np.arange(2 * WIN_C - 1)[:, None, None]).astype(np.float32)
    bt = jnp.einsum('hrd,dcj->hrcj', table.astype(F32), jnp.asarray(onehot),
                    precision=lax.Precision.HIGHEST)
    bt = jnp.where(col_mask[None, None], bt * LOG2_E, NEG_INF)
    b = jnp.concatenate([bt[:, :-1], bt[:, 1:]], axis=-1)
    b = b.reshape(N_PAIRS, 2, 2 * WIN_R - 2, GRID_W, 2 * GRID_W).transpose(0, 2, 1, 3, 4)
    return b.reshape(BIAS_SHAPE)


def _ada_kernel(c_ref, w_ref, b_ref, o_ref):
    c = c_ref[...]
    a = (c * jax.nn.sigmoid(c)).astype(BF16)
    o_ref[0] = _dot(a, w_ref[0].astype(BF16)) + b_ref[0]


def _ada_mod(cond, w_ada, b_ada):
    tn = 1536
    out = pl.pallas_call(
        _ada_kernel,
        grid=(DEPTH, 6 * D_MODEL // tn),
        in_specs=[_resident((8, D_MODEL), lambda l, j: (0, 0)),
                  pl.BlockSpec((1, D_MODEL, tn), lambda l, j: (l, 0, j)),
                  pl.BlockSpec((1, 1, tn), lambda l, j: (l, 0, j))],
        out_specs=pl.BlockSpec((1, 8, tn), lambda l, j: (l, 0, j)),
        out_shape=jax.ShapeDtypeStruct((DEPTH, 8, 6 * D_MODEL), F32),
        compiler_params=_params(2),
        name="ada_mod",
    )(cond, w_ada, b_ada.reshape(DEPTH, 1, 6 * D_MODEL))
    return out.reshape(DEPTH, 8, 6, D_MODEL)


def _rms(x):
    return x * lax.rsqrt(jnp.mean(x * x, axis=-1, keepdims=True) + EPS)


def _inproj_kernel(x_ref, mod_ref, g_ref, w_ref, up_ref, uf_ref, qkv_ref):
    y = _rms(x_ref[...]) * g_ref[...]
    h = (y * (1.0 + mod_ref[0, 1:2, :]) + mod_ref[0, 0:1, :]).astype(BF16)
    up_ref[...] = _dot(h, w_ref[:, 0:256]).astype(BF16)
    uf_ref[...] = _dot(h, w_ref[:, 256:512]).astype(BF16)
    qkv_ref[:, 0:512] = (_dot(h, w_ref[:, 512:1024]) * Q_SCALE).astype(BF16)
    qkv_ref[:, 512:1024] = _dot(h, w_ref[:, 1024:1536]).astype(BF16)
    qkv_ref[:, 1024:1536] = _dot(h, w_ref[:, 1536:2048]).astype(BF16)


def _layer_weight(layer, rows, cols):
    return _resident((None, rows, cols), lambda i: (layer, 0, 0))


def _inproj(x, mod, g, w_in, layer, mod_row, tm):
    n = x.shape[0]
    return pl.pallas_call(
        _inproj_kernel,
        grid=(n // tm,),
        in_specs=[pl.BlockSpec((tm, D_MODEL), lambda i: (i, 0)),
                  pl.BlockSpec((1, 6, D_MODEL), lambda i: (mod_row(i, tm), 0, 0)),
                  _resident((1, D_MODEL), lambda i: (0, 0)),
                  _layer_weight(layer, D_MODEL, 2048)],
        out_specs=[pl.BlockSpec((tm, 256), lambda i: (i, 0)),
                   pl.BlockSpec((tm, 256), lambda i: (i, 0)),
                   pl.BlockSpec((tm, 1536), lambda i: (i, 0))],
        out_shape=[jax.ShapeDtypeStruct((n, 256), BF16),
                   jax.ShapeDtypeStruct((n, 256), BF16),
                   jax.ShapeDtypeStruct((n, 1536), BF16)],
        compiler_params=_params(1),
        name="inproj",
    )(x, mod, g, w_in)


POOL_SUB = 256
POOL_BAND_K = POOL_SUB + 2 * POOL_HALO


@functools.lru_cache(maxsize=None)
def _pool_bands():
    t = np.arange(POOL_SUB)[:, None]
    d = np.arange(POOL_BAND_K)[None, :] - POOL_HALO - t
    member = [((d >= -(w // 2)) & (d < w - w // 2)).astype(np.float64) for w in POOL_WINDOWS]
    centred = [m / w - (d == 0) for m, w in zip(member, POOL_WINDOWS)]
    interior = np.stack([np.concatenate(centred[0:2], axis=1), np.concatenate(centred[2:4], axis=1)], axis=0)
    as_bf16 = lambda a: jnp.asarray(a, dtype=F32).astype(BF16)
    return as_bf16(np.concatenate(member, axis=0)), as_bf16(interior)


def _pool_kernel(band_ref, inner_ref, prev_ref, cur_ref, next_ref, pw_ref, ps_ref, o_ref, *, seq_tiles, nsub):
    st = pl.program_id(0) % seq_tiles
    ub_all = jnp.concatenate([prev_ref[...], cur_ref[...], next_ref[...]], axis=0)
    left = lax.broadcasted_iota(jnp.int32, (POOL_BAND_K, 128), 1) < GROUP_WIDTH

    for s in range(nsub):
        ub = ub_all[s * POOL_SUB:s * POOL_SUB + POOL_BAND_K]
        cur = ub[POOL_HALO:POOL_HALO + POOL_SUB]

        def interior(ub=ub):
            halves = []
            for h in range(2):
                t = ub[:, 128 * h:128 * (h + 1)]
                zero = jnp.zeros_like(t)
                stacked = jnp.concatenate([jnp.where(left, t, zero), jnp.where(left, zero, t)], axis=0)
                halves.append(_dot(inner_ref[h], stacked))
            return jnp.concatenate(halves, axis=1).astype(BF16)

        def clipped(ub=ub, cur=cur, s=s):
            lo_b = jnp.where(st == 0, 0, -POOL_HALO) if s == 0 else -POOL_HALO
            hi_b = (jnp.where(st == seq_tiles - 1, POOL_SUB, POOL_SUB + POOL_HALO) if s == nsub - 1
                    else POOL_SUB + POOL_HALO)
            p = lax.broadcasted_iota(jnp.int32, ub.shape, 0) - POOL_HALO
            ubv = jnp.where((p >= lo_b) & (p < hi_b), ub, jnp.zeros_like(ub))
            sums = _dot(band_ref[...], ubv)
            tcol = lax.broadcasted_iota(jnp.int32, (POOL_SUB, 1), 0)
            group = lax.broadcasted_iota(jnp.int32, (POOL_SUB, POOL_WIDTH), 1) // GROUP_WIDTH
            mean = jnp.zeros((POOL_SUB, POOL_WIDTH), F32)
            for g, w in enumerate(POOL_WINDOWS):
                cnt = jnp.minimum(tcol + (w - w // 2), hi_b) - jnp.maximum(tcol - w // 2, lo_b)
                mean = jnp.where(group == g, sums[g * POOL_SUB:(g + 1) * POOL_SUB] / cnt.astype(F32), mean)
            return (mean - cur.astype(F32)).astype(BF16)

        at_start = (st == 0) if s == 0 else None
        at_end = (st == seq_tiles - 1) if s == nsub - 1 else None
        if at_start is None and at_end is None:
            diff = interior()
        else:
            edge = at_start if at_end is None else (at_end if at_start is None else at_start | at_end)
            diff = lax.cond(edge, clipped, interior)
        o_ref[s * POOL_SUB:(s + 1) * POOL_SUB] = (_dot(diff, pw_ref[...]) * ps_ref[...]).astype(BF16)


def _pool(up, pool_w_bd, pool_scale, seq_len, tm):
    n = up.shape[0]
    hb = tm // POOL_HALO
    n_halo_blocks = n // POOL_HALO
    bands, inner = _pool_bands()
    return pl.pallas_call(
        functools.partial(_pool_kernel, seq_tiles=seq_len // tm, nsub=tm // POOL_SUB),
        grid=(n // tm,),
        in_specs=[_resident(bands.shape, lambda i: (0, 0)),
                  _resident(inner.shape, lambda i: (0, 0, 0)),
                  pl.BlockSpec((POOL_HALO, POOL_WIDTH), lambda i: (jnp.maximum(i * hb - 1, 0), 0)),
                  pl.BlockSpec((tm, POOL_WIDTH), lambda i: (i, 0)),
                  pl.BlockSpec((POOL_HALO, POOL_WIDTH),
                               lambda i: (jnp.minimum((i + 1) * hb, n_halo_blocks - 1), 0)),
                  _resident((POOL_WIDTH, POOL_WIDTH), lambda i: (0, 0)),
                  _resident((1, POOL_WIDTH), lambda i: (0, 0))],
        out_specs=pl.BlockSpec((tm, POOL_WIDTH), lambda i: (i, 0)),
        out_shape=jax.ShapeDtypeStruct((n, POOL_WIDTH), BF16),
        compiler_params=_params(1),
        name="pool",
    )(bands, inner, up, up, up, pool_w_bd, pool_scale)


FFT_CHUNK = 16


def _fft1_kernel(m_ref, x_ref, y_ref):
    xt = jnp.transpose(x_ref[...], (1, 0, 2))
    y = jnp.stack([_dot(m_ref[...], xt[n1]).astype(BF16) for n1 in range(FFT_CHUNK)], axis=0)
    y_ref[...] = jnp.transpose(y, (1, 0, 2)).reshape(y_ref.shape)


def _fourier_epilogue(zz, cs_ref, w_ref, norm):
    f = (_dot(zz, cs_ref[...]) * norm).astype(BF16)
    return _dot(f, w_ref[...])


def _fft2_kernel(t_ref, y_ref, cs_ref, w_ref, o_ref, zz, *, norm):
    for kk in range(FFT_CHUNK):
        z = _dot(t_ref[kk], y_ref[kk].reshape(2 * FFT_N1, FOURIER_WIDTH))
        zz[kk * FFT_N1:(kk + 1) * FFT_N1, :] = jnp.concatenate([z[:FFT_N1], z[FFT_N1:]], axis=1).astype(BF16)
    o = _fourier_epilogue(zz[...], cs_ref, w_ref, norm).astype(BF16)
    o_ref[...] = jnp.transpose(o.reshape(FFT_CHUNK, FFT_N1, FOURIER_WIDTH), (1, 0, 2))


def _fourier_latent(uf, fourier_w_bd):
    stage1, stage2, _, chan = _dft_constants()
    chunks1 = FFT_N1 // FFT_CHUNK
    y = pl.pallas_call(
        _fft1_kernel,
        grid=(BATCH, chunks1),
        in_specs=[_resident((2 * FFT_N2, FFT_N2), lambda b, j: (0, 0)),
                  pl.BlockSpec((FFT_N2, FFT_CHUNK, FOURIER_WIDTH), lambda b, j: (b, j, 0))],
        out_specs=pl.BlockSpec((FFT_N2, 2, FFT_CHUNK, FOURIER_WIDTH), lambda b, j: (b, 0, j, 0)),
        out_shape=jax.ShapeDtypeStruct((BATCH * FFT_N2, 2, FFT_N1, FOURIER_WIDTH), BF16),
        compiler_params=_params(2),
        name="fft_stage1",
    )(stage1, uf.reshape(BATCH * FFT_N2, FFT_N1, FOURIER_WIDTH))
    chunks2 = FFT_N2 // FFT_CHUNK
    out = pl.pallas_call(
        functools.partial(_fft2_kernel, norm=float((SEQ * GROUP_WIDTH) ** -0.5)),
        grid=(chunks2, BATCH),
        in_specs=[pl.BlockSpec((FFT_CHUNK, 2 * FFT_N1, 2 * FFT_N1), lambda j, b: (j, 0, 0)),
                  pl.BlockSpec((FFT_CHUNK, 2, FFT_N1, FOURIER_WIDTH), lambda j, b: (b * chunks2 + j, 0, 0, 0)),
                  _resident((2 * FOURIER_WIDTH, FOURIER_WIDTH), lambda j, b: (0, 0)),
                  _resident((FOURIER_WIDTH, FOURIER_WIDTH), lambda j, b: (0, 0))],
        out_specs=pl.BlockSpec((FFT_N1, FFT_CHUNK, FOURIER_WIDTH), lambda j, b: (b, j, 0)),
        out_shape=jax.ShapeDtypeStruct((BATCH * FFT_N1, FFT_N2, FOURIER_WIDTH), BF16),
        scratch_shapes=[pltpu.VMEM((FFT_CHUNK * FFT_N1, 2 * FOURIER_WIDTH), BF16)],
        compiler_params=_params(2),
        name="fft_stage2",
    )(stage2, y, chan, fourier_w_bd)
    return out.reshape(BATCH * SEQ, FOURIER_WIDTH)


def _fourier_ctx_kernel(m_ref, u_ref, cs_ref, w_ref, o_ref, *, norm):
    z = _dot(m_ref[...], u_ref[...])
    zz = jnp.concatenate([z[:CTX_LEN], z[CTX_LEN:]], axis=1).astype(BF16)
    o_ref[...] = _fourier_epilogue(zz, cs_ref, w_ref, norm).astype(BF16)


def _fourier_ctx(ufc, fourier_w_bd):
    _, _, ctx_dft, chan = _dft_constants()
    return pl.pallas_call(
        functools.partial(_fourier_ctx_kernel, norm=float((CTX_LEN * GROUP_WIDTH) ** -0.5)),
        grid=(BATCH,),
        in_specs=[_resident((2 * CTX_LEN, CTX_LEN), lambda b: (0, 0)),
                  pl.BlockSpec((CTX_LEN, FOURIER_WIDTH), lambda b: (b, 0)),
                  _resident((2 * FOURIER_WIDTH, FOURIER_WIDTH), lambda b: (0, 0)),
                  _resident((FOURIER_WIDTH, FOURIER_WIDTH), lambda b: (0, 0))],
        out_specs=pl.BlockSpec((CTX_LEN, FOURIER_WIDTH), lambda b: (b, 0)),
        out_shape=jax.ShapeDtypeStruct((BATCH * CTX_LEN, FOURIER_WIDTH), BF16),
        compiler_params=_params(1),
        name="fourier_ctx",
    )(ctx_dft, ufc, chan, fourier_w_bd)


def _split_pair(x, even):
    zero = jnp.zeros_like(x)
    return jnp.concatenate([jnp.where(even, x, zero), jnp.where(even, zero, x)], axis=0)


def _softmax_pv(scores, values, n_q, even):
    m = functools.reduce(jnp.maximum, [jnp.max(s, axis=1, keepdims=True) for s in scores])
    e = [jnp.exp2(s - m) for s in scores]
    l = functools.reduce(jnp.add, [jnp.sum(x, axis=1, keepdims=True) for x in e])
    o = functools.reduce(jnp.add, [_dot(x.astype(BF16), v) for x, v in zip(e, values)])
    o = o / l
    return jnp.where(even, o[:n_q], o[n_q:])


ATTN_TILE_ROWS = 16
ATTN_TILE = ATTN_TILE_ROWS * GRID_W
ATTN_HALO = WIN_R * GRID_W
SOFTMAX_ROWS = 32


def _kv_window_row(t):
    return jnp.clip(t * ATTN_TILE_ROWS - WIN_R, 0, GRID_ROWS - ATTN_TILE_ROWS - 2 * WIN_R)


def _nat_kernel(q_ref, kbuf, vbuf, kx_ref, vx_ref, bias_ref, o_ref, sbuf_a, sbuf_b, pbuf):
    t = pl.program_id(1)
    even = lax.broadcasted_iota(jnp.int32, (GRID_W, 128), 1) < HEAD_DIM
    n_win = WIN_R * GRID_W

    def window_offset(j):
        r = t * ATTN_TILE_ROWS + j
        rs = jnp.clip(r - WIN_R // 2, 0, GRID_ROWS - WIN_R)
        return pl.multiple_of((rs - _kv_window_row(t)) * GRID_W, GRID_W)

    def query_offset(j):
        return j * GRID_W if isinstance(j, int) else pl.multiple_of(j * GRID_W, GRID_W)

    def scores(j, sbuf):
        r = t * ATTN_TILE_ROWS + j
        off = window_offset(j)
        dr0 = jnp.clip(r - WIN_R // 2, 0, GRID_ROWS - WIN_R) - r + WIN_R - 1
        qoff = query_offset(j)
        for p in range(N_PAIRS):
            lanes = slice(128 * p, 128 * (p + 1))
            qs = _split_pair(q_ref[pl.ds(qoff, GRID_W), lanes], even)
            bias = jnp.concatenate([bias_ref[p, dr0 + 2 * m] for m in range(WIN_R // 2)], axis=1)
            sbuf[p, :, 0:n_win] = _dot_nt(qs, kbuf[pl.ds(off, n_win), lanes]) + bias
            sbuf[p, :, n_win:] = _dot_nt(qs, kx_ref[:, lanes])

    def outputs(j, sbuf):
        off = window_offset(j)
        qoff = query_offset(j)
        for p in range(N_PAIRS):
            lanes = slice(128 * p, 128 * (p + 1))
            sums = []
            for g in range(2 * GRID_W // SOFTMAX_ROWS):
                rows = slice(g * SOFTMAX_ROWS, (g + 1) * SOFTMAX_ROWS)
                s = sbuf[p, rows, :]
                e = jnp.exp2(s - jnp.max(s, axis=1, keepdims=True))
                sums.append(jnp.sum(e, axis=1, keepdims=True))
                pbuf[p, rows, :] = e.astype(BF16)
            l = jnp.concatenate(sums, axis=0)
            o = (_dot(pbuf[p, :, :n_win], vbuf[pl.ds(off, n_win), lanes])
                 + _dot(pbuf[p, :, n_win:], vx_ref[:, lanes]))
            o = o / l
            o_ref[pl.ds(qoff, GRID_W), lanes] = jnp.where(even, o[:GRID_W], o[GRID_W:]).astype(BF16)

    scores(0, sbuf_a)

    def body(i, carry):
        scores(2 * i + 1, sbuf_b)
        outputs(2 * i, sbuf_a)
        scores(jnp.minimum(2 * i + 2, ATTN_TILE_ROWS - 1), sbuf_a)
        outputs(2 * i + 1, sbuf_b)
        return carry

    lax.fori_loop(0, ATTN_TILE_ROWS // 2, body, 0)


def _neighbourhood_attention(qkv, qkv_ctx, bias):
    tiles = SEQ // ATTN_TILE
    cur = lambda col: (lambda b, t: (b * tiles + t, col))
    blk = lambda f: pl.BlockSpec((ATTN_TILE, ATTN_WIDTH), f)
    kv = lambda col: pl.BlockSpec(
        (pl.Element(ATTN_TILE + 2 * ATTN_HALO), pl.Element(ATTN_WIDTH)),
        lambda b, t: (pl.multiple_of(b * SEQ + _kv_window_row(t) * GRID_W, GRID_W), col * ATTN_WIDTH))
    return pl.pallas_call(
        _nat_kernel,
        grid=(BATCH, tiles),
        in_specs=[blk(cur(0)), kv(1), kv(2),
                  pl.BlockSpec((CTX_LEN, ATTN_WIDTH), lambda b, t: (b, 1)),
                  pl.BlockSpec((CTX_LEN, ATTN_WIDTH), lambda b, t: (b, 2)),
                  _resident(bias.shape, lambda b, t: (0, 0, 0, 0))],
        out_specs=blk(cur(0)),
        out_shape=jax.ShapeDtypeStruct((BATCH * SEQ, ATTN_WIDTH), BF16),
        scratch_shapes=[pltpu.VMEM((N_PAIRS, 2 * GRID_W, WIN_R * GRID_W + CTX_LEN), F32),
                        pltpu.VMEM((N_PAIRS, 2 * GRID_W, WIN_R * GRID_W + CTX_LEN), F32),
                        pltpu.VMEM((N_PAIRS, 2 * GRID_W, WIN_R * GRID_W + CTX_LEN), BF16)],
        compiler_params=_params(2),
        name="nat_attention",
    )(qkv, qkv, qkv, qkv_ctx, qkv_ctx, bias)


def _ctx_attn_kernel(q_ref, k_ref, v_ref, o_ref):
    even = lax.broadcasted_iota(jnp.int32, (CTX_LEN, 128), 1) < HEAD_DIM
    for p in range(N_PAIRS):
        lanes = slice(128 * p, 128 * (p + 1))
        qs = _split_pair(q_ref[:, lanes], even)
        s = _dot_nt(qs, k_ref[:, lanes])
        o_ref[:, lanes] = _softmax_pv([s], [v_ref[:, lanes]], CTX_LEN, even).astype(BF16)


def _context_attention(qkv_ctx):
    blk = lambda col: pl.BlockSpec((CTX_LEN, ATTN_WIDTH), lambda b: (b, col))
    return pl.pallas_call(
        _ctx_attn_kernel,
        grid=(BATCH,),
        in_specs=[blk(0), blk(1), blk(2)],
        out_specs=blk(0),
        out_shape=jax.ShapeDtypeStruct((BATCH * CTX_LEN, ATTN_WIDTH), BF16),
        compiler_params=_params(1),
        name="ctx_attention",
    )(qkv_ctx, qkv_ctx, qkv_ctx)


FFN_CHUNK = D_FF // 2


def _outffn_kernel(x_ref, yp_ref, yf_ref, at_ref, mod_ref, wo_ref, g2_ref, w1_ref, w3_ref, w2_ref,
                   *rest, final):
    o_ref = rest[-1]
    y = jnp.concatenate([yp_ref[...], yf_ref[...], at_ref[...]], axis=1)
    x1 = x_ref[...] + mod_ref[0, 2:3, :] * _dot(y, wo_ref[...])
    h = (_rms(x1) * g2_ref[...] * (1.0 + mod_ref[0, 4:5, :]) + mod_ref[0, 3:4, :]).astype(BF16)
    ffn = jnp.zeros_like(x1)
    for c in range(D_FF // FFN_CHUNK):
        cols = slice(c * FFN_CHUNK, (c + 1) * FFN_CHUNK)
        a = _dot(h, w1_ref[:, cols])
        gated = (a * jax.nn.sigmoid(a) * _dot(h, w3_ref[:, cols])).astype(BF16)
        ffn = ffn + _dot(gated, w2_ref[cols, :])
    x2 = x1 + mod_ref[0, 5:6, :] * ffn
    if final:
        x2 = _rms(x2) * rest[0][...]
    o_ref[...] = x2


def _outffn(x, yp, yf, at, mod, w_out, g2, w1, w3, w2, final_g, layer, mod_row, tm):
    n = x.shape[0]
    row = lambda w: pl.BlockSpec((tm, w), lambda i: (i, 0))
    in_specs = [row(D_MODEL), row(256), row(256), row(512),
                pl.BlockSpec((1, 6, D_MODEL), lambda i: (mod_row(i, tm), 0, 0)),
                _layer_weight(layer, D_MODEL, D_MODEL),
                _resident((1, D_MODEL), lambda i: (0, 0)),
                _layer_weight(layer, D_MODEL, D_FF),
                _layer_weight(layer, D_MODEL, D_FF),
                _layer_weight(layer, D_FF, D_MODEL)]
    args = [x, yp, yf, at, mod, w_out, g2, w1, w3, w2]
    if final_g is not None:
        in_specs.append(_resident((1, D_MODEL), lambda i: (0, 0)))
        args.append(final_g)
    return pl.pallas_call(
        functools.partial(_outffn_kernel, final=final_g is not None),
        grid=(n // tm,),
        in_specs=in_specs,
        out_specs=row(D_MODEL),
        out_shape=jax.ShapeDtypeStruct((n, D_MODEL), F32),
        compiler_params=_params(1),
        name="outproj_ffn",
    )(*args)


def _latent_mod_row(i, tm):
    return (i * tm) // SEQ


def _ctx_mod_row(i, tm):
    return BATCH


def kernel(x, c, ctx, c_ctx, w_ada, b_ada, norm1_g, w_in, pool_w, pool_scale, fourier_w, nat_bias, w_out,
           norm2_g, w_ffn1, w_ffn3, w_ffn2, final_g):
    cond = jnp.concatenate([c, c_ctx[None, :], jnp.zeros((8 - BATCH - 1, D_MODEL), F32)], axis=0)
    mods = _ada_mod(cond, w_ada, b_ada)
    x = x.reshape(BATCH * SEQ, D_MODEL)
    ctx = ctx.reshape(BATCH * CTX_LEN, D_MODEL)
    w_in, w_out, w_ffn1, w_ffn3, w_ffn2 = (w.astype(BF16) for w in (w_in, w_out, w_ffn1, w_ffn3, w_ffn2))
    for l in range(DEPTH):
        last = l == DEPTH - 1
        mod = mods[l]
        g1 = norm1_g[l][None, :]
        g2 = norm2_g[l][None, :]
        pool_w_bd = _group_block_diag(pool_w[l])
        fourier_w_bd = _group_block_diag(fourier_w[l])
        scale = pool_scale[l][None, :]
        ffn_w = (w_out, g2, w_ffn1, w_ffn3, w_ffn2)

        upc, ufc, qkvc = _inproj(ctx, mod, g1, w_in, l, _ctx_mod_row, 256)
        up, uf, qkv = _inproj(x, mod, g1, w_in, l, _latent_mod_row, 1024)
        yp = _pool(up, pool_w_bd, scale, SEQ, 1024)
        yf = _fourier_latent(uf, fourier_w_bd)
        at = _neighbourhood_attention(qkv, qkvc, _attention_bias(nat_bias[l]))
        x = _outffn(x, yp, yf, at, mod, *ffn_w, final_g[None, :] if last else None, l, _latent_mod_row, 512)
        if not last:
            ypc = _pool(upc, pool_w_bd, scale, CTX_LEN, CTX_LEN)
            yfc = _fourier_ctx(ufc, fourier_w_bd)
            atc = _context_attention(qkvc)
            ctx = _outffn(ctx, ypc, yfc, atc, mod, *ffn_w, None, l, _ctx_mod_row, 256)
    return x.reshape(BATCH, SEQ, D_MODEL)
```

```python
import functools

import numpy as np
import jax
import jax.numpy as jnp
from jax import lax
from jax.experimental import pallas as pl
from jax.experimental.pallas import tpu as pltpu

D_MODEL = 1024
BATCH = 4
SEQ = 8192
DEPTH = 4
GRID_W = 64
GRID_ROWS = SEQ // GRID_W
CTX_LEN = 256
POOL_WINDOWS = (2, 4, 8, 16)
POOL_WIDTH = 256
FOURIER_WIDTH = 256
GROUP_WIDTH = 64
HEAD_DIM = 64
N_HEADS = 8
N_PAIRS = N_HEADS // 2
ATTN_WIDTH = 512
WIN_R = 8
WIN_C = 16
D_FF = 2816
EPS = 1e-6
NEG_INF = -1e30
LOG2_E = 1.4426950408889634
Q_SCALE = HEAD_DIM ** -0.5 * LOG2_E

FFT_N1 = 128
FFT_N2 = 64
POOL_HALO = 16
BIAS_SHAPE = (N_PAIRS, 2 * WIN_R - 2, 2 * GRID_W, 2 * GRID_W)

V7X_VMEM_LIMIT = 56 * 1024 * 1024

F32 = jnp.float32
BF16 = jnp.bfloat16


def _dot(a, b):
    return jnp.dot(a, b, preferred_element_type=F32)


def _dot_nt(a, b):
    return lax.dot_general(a, b, (((1,), (1,)), ((), ())), preferred_element_type=F32)


def _params(n_grid_dims):
    return pltpu.CompilerParams(dimension_semantics=("arbitrary",) * n_grid_dims,
                                vmem_limit_bytes=V7X_VMEM_LIMIT)


def _resident(block_shape, index_map):
    return pl.BlockSpec(block_shape, index_map, pipeline_mode=pl.Buffered(1))


def _block_diag(m, groups):
    w = m.shape[0]
    out = np.zeros((groups * w, groups * w), m.dtype)
    for g in range(groups):
        out[g * w:(g + 1) * w, g * w:(g + 1) * w] = m
    return out


@functools.lru_cache(maxsize=None)
def _dft_constants():
    two_pi = 2.0 * np.pi
    a2 = two_pi * (np.outer(np.arange(FFT_N2), np.arange(FFT_N2)) % FFT_N2) / FFT_N2
    stage1 = np.stack([np.cos(a2), -np.sin(a2)], axis=1).reshape(2 * FFT_N2, FFT_N2)
    k = FFT_N2 * np.arange(FFT_N1)[None, :, None] + np.arange(FFT_N2)[:, None, None]
    n1 = np.arange(FFT_N1)[None, None, :]
    ang = two_pi * ((k * n1) % SEQ) / SEQ
    mr, mi = np.cos(ang), -np.sin(ang)
    stage2 = np.concatenate([np.concatenate([mr, -mi], axis=2),
                             np.concatenate([mi, mr], axis=2)], axis=1)
    ac = two_pi * (np.outer(np.arange(CTX_LEN), np.arange(CTX_LEN)) % CTX_LEN) / CTX_LEN
    ctx_dft = np.concatenate([np.cos(ac), -np.sin(ac)], axis=0)
    ag = two_pi * (np.outer(np.arange(GROUP_WIDTH), np.arange(GROUP_WIDTH)) % GROUP_WIDTH) / GROUP_WIDTH
    chan = np.concatenate([_block_diag(np.cos(ag), 4), _block_diag(np.sin(ag), 4)], axis=0)
    as_bf16 = lambda a: jnp.asarray(a, dtype=F32).astype(BF16)
    return as_bf16(stage1), as_bf16(stage2), as_bf16(ctx_dft), as_bf16(chan)


def _group_block_diag(w):
    out = jnp.zeros((4, GROUP_WIDTH, 4, GROUP_WIDTH), w.dtype)
    for g in range(4):
        out = out.at[g, :, g, :].set(w[g])
    return out.reshape(4 * GROUP_WIDTH, 4 * GROUP_WIDTH).astype(BF16)


def _attention_bias(table):
    col = np.arange(GRID_W)
    cs = np.clip(col - WIN_C // 2, 0, GRID_W - WIN_C)
    col_mask = (col[None, :] >= cs[:, None]) & (col[None, :] < cs[:, None] + WIN_C)
    dc_idx = np.clip(col[None, :] - col[:, None] + WIN_C - 1, 0, 2 * WIN_C - 2)
    onehot = (dc_idx[None] == np.arange(2 * WIN_C - 1)[:, None, None]).astype(np.float32)
    bt = jnp.einsum('hrd,dcj->hrcj', table.astype(F32), jnp.asarray(onehot),
                    precision=lax.Precision.HIGHEST)
    bt = jnp.where(col_mask[None, None], bt * LOG2_E, NEG_INF)
    b = jnp.concatenate([bt[:, :-1], bt[:, 1:]], axis=-1)
    b = b.reshape(N_PAIRS, 2, 2 * WIN_R - 2, GRID_W, 2 * GRID_W).transpose(0, 2, 1, 3, 4)
    return b.reshape(BIAS_SHAPE)


def _ada_kernel(c_ref, w_ref, b_ref, o_ref):
    c = c_ref[...]
    a = (c * jax.nn.sigmoid(c)).astype(BF16)
    o_ref[0] = _dot(a, w_ref[0].astype(BF16)) + b_ref[0]


def _ada_mod(cond, w_ada, b_ada):
    tn = 1536
    out = pl.pallas_call(
        _ada_kernel,
        grid=(DEPTH, 6 * D_MODEL // tn),
        in_specs=[_resident((8, D_MODEL), lambda l, j: (0, 0)),
                  pl.BlockSpec((1, D_MODEL, tn), lambda l, j: (l, 0, j)),
                  pl.BlockSpec((1, 1, tn), lambda l, j: (l, 0, j))],
        out_specs=pl.BlockSpec((1, 8, tn), lambda l, j: (l, 0, j)),
        out_shape=jax.ShapeDtypeStruct((DEPTH, 8, 6 * D_MODEL), F32),
        compiler_params=_params(2),
        name="ada_mod",
    )(cond, w_ada, b_ada.reshape(DEPTH, 1, 6 * D_MODEL))
    return out.reshape(DEPTH, 8, 6, D_MODEL)


def _rms(x):
    return x * lax.rsqrt(jnp.mean(x * x, axis=-1, keepdims=True) + EPS)


def _inproj_kernel(x_ref, mod_ref, g_ref, w_ref, up_ref, uf_ref, qkv_ref):
    y = _rms(x_ref[...]) * g_ref[...]
    h = (y * (1.0 + mod_ref[0, 1:2, :]) + mod_ref[0, 0:1, :]).astype(BF16)
    up_ref[...] = _dot(h, w_ref[:, 0:256]).astype(BF16)
    uf_ref[...] = _dot(h, w_ref[:, 256:512]).astype(BF16)
    qkv_ref[:, 0:512] = (_dot(h, w_ref[:, 512:1024]) * Q_SCALE).astype(BF16)
    qkv_ref[:, 512:1024] = _dot(h, w_ref[:, 1024:1536]).astype(BF16)
    qkv_ref[:, 1024:1536] = _dot(h, w_ref[:, 1536:2048]).astype(BF16)


def _layer_weight(layer, rows, cols):
    return _resident((None, rows, cols), lambda i: (layer, 0, 0))


def _inproj(x, mod, g, w_in, layer, mod_row, tm):
    n = x.shape[0]
    return pl.pallas_call(
        _inproj_kernel,
        grid=(n // tm,),
        in_specs=[pl.BlockSpec((tm, D_MODEL), lambda i: (i, 0)),
                  pl.BlockSpec((1, 6, D_MODEL), lambda i: (mod_row(i, tm), 0, 0)),
                  _resident((1, D_MODEL), lambda i: (0, 0)),
                  _layer_weight(layer, D_MODEL, 2048)],
        out_specs=[pl.BlockSpec((tm, 256), lambda i: (i, 0)),
                   pl.BlockSpec((tm, 256), lambda i: (i, 0)),
                   pl.BlockSpec((tm, 1536), lambda i: (i, 0))],
        out_shape=[jax.ShapeDtypeStruct((n, 256), BF16),
                   jax.ShapeDtypeStruct((n, 256), BF16),
                   jax.ShapeDtypeStruct((n, 1536), BF16)],
        compiler_params=_params(1),
        name="inproj",
    )(x, mod, g, w_in)


POOL_SUB = 256
POOL_BAND_K = POOL_SUB + 2 * POOL_HALO


@functools.lru_cache(maxsize=None)
def _pool_bands():
    t = np.arange(POOL_SUB)[:, None]
    d = np.arange(POOL_BAND_K)[None, :] - POOL_HALO - t
    member = [((d >= -(w // 2)) & (d < w - w // 2)).astype(np.float64) for w in POOL_WINDOWS]
    centred = [m / w - (d == 0) for m, w in zip(member, POOL_WINDOWS)]
    interior = np.stack([np.concatenate(centred[0:2], axis=1), np.concatenate(centred[2:4], axis=1)], axis=0)
    as_bf16 = lambda a: jnp.asarray(a, dtype=F32).astype(BF16)
    return as_bf16(np.concatenate(member, axis=0)), as_bf16(interior)


def _pool_kernel(band_ref, inner_ref, prev_ref, cur_ref, next_ref, pw_ref, ps_ref, o_ref, *, seq_tiles, nsub):
    st = pl.program_id(0) % seq_tiles
    ub_all = jnp.concatenate([prev_ref[...], cur_ref[...], next_ref[...]], axis=0)
    left = lax.broadcasted_iota(jnp.int32, (POOL_BAND_K, 128), 1) < GROUP_WIDTH

    for s in range(nsub):
        ub = ub_all[s * POOL_SUB:s * POOL_SUB + POOL_BAND_K]
        cur = ub[POOL_HALO:POOL_HALO + POOL_SUB]

        def interior(ub=ub):
            halves = []
            for h in range(2):
                t = ub[:, 128 * h:128 * (h + 1)]
                zero = jnp.zeros_like(t)
                stacked = jnp.concatenate([jnp.where(left, t, zero), jnp.where(left, zero, t)], axis=0)
                halves.append(_dot(inner_ref[h], stacked))
            return jnp.concatenate(halves, axis=1).astype(BF16)

        def clipped(ub=ub, cur=cur, s=s):
            lo_b = jnp.where(st == 0, 0, -POOL_HALO) if s == 0 else -POOL_HALO
            hi_b = (jnp.where(st == seq_tiles - 1, POOL_SUB, POOL_SUB + POOL_HALO) if s == nsub - 1
                    else POOL_SUB + POOL_HALO)
            p = lax.broadcasted_iota(jnp.int32, ub.shape, 0) - POOL_HALO
            ubv = jnp.where((p >= lo_b) & (p < hi_b), ub, jnp.zeros_like(ub))
            sums = _dot(band_ref[...], ubv)
            tcol = lax.broadcasted_iota(jnp.int32, (POOL_SUB, 1), 0)
            group = lax.broadcasted_iota(jnp.int32, (POOL_SUB, POOL_WIDTH), 1) // GROUP_WIDTH
            mean = jnp.zeros((POOL_SUB, POOL_WIDTH), F32)
            for g, w in enumerate(POOL_WINDOWS):
                cnt = jnp.minimum(tcol + (w - w // 2), hi_b) - jnp.maximum(tcol - w // 2, lo_b)
                mean = jnp.where(group == g, sums[g * POOL_SUB:(g + 1) * POOL_SUB] / cnt.astype(F32), mean)
            return (mean - cur.astype(F32)).astype(BF16)

        at_start = (st == 0) if s == 0 else None
        at_end = (st == seq_tiles - 1) if s == nsub - 1 else None
        if at_start is None and at_end is None:
            diff = interior()
        else:
            edge = at_start if at_end is None else (at_end if at_start is None else at_start | at_end)
            diff = lax.cond(edge, clipped, interior)
        o_ref[s * POOL_SUB:(s + 1) * POOL_SUB] = (_dot(diff, pw_ref[...]) * ps_ref[...]).astype(BF16)


def _pool(up, pool_w_bd, pool_scale, seq_len, tm):
    n = up.shape[0]
    hb = tm // POOL_HALO
    n_halo_blocks = n // POOL_HALO
    bands, inner = _pool_bands()
    return pl.pallas_call(
        functools.partial(_pool_kernel, seq_tiles=seq_len // tm, nsub=tm // POOL_SUB),
        grid=(n // tm,),
        in_specs=[_resident(bands.shape, lambda i: (0, 0)),
                  _resident(inner.shape, lambda i: (0, 0, 0)),
                  pl.BlockSpec((POOL_HALO, POOL_WIDTH), lambda i: (jnp.maximum(i * hb - 1, 0), 0)),
                  pl.BlockSpec((tm, POOL_WIDTH), lambda i: (i, 0)),
                  pl.BlockSpec((POOL_HALO, POOL_WIDTH),
                               lambda i: (jnp.minimum((i + 1) * hb, n_halo_blocks - 1), 0)),
                  _resident((POOL_WIDTH, POOL_WIDTH), lambda i: (0, 0)),
                  _resident((1, POOL_WIDTH), lambda i: (0, 0))],
        out_specs=pl.BlockSpec((tm, POOL_WIDTH), lambda i: (i, 0)),
        out_shape=jax.ShapeDtypeStruct((n, POOL_WIDTH), BF16),
        compiler_params=_params(1),
        name="pool",
    )(bands, inner, up, up, up, pool_w_bd, pool_scale)


FFT_CHUNK = 16


def _fft1_kernel(m_ref, x_ref, y_ref):
    xt = jnp.transpose(x_ref[...], (1, 0, 2))
    y = jnp.stack([_dot(m_ref[...], xt[n1]).astype(BF16) for n1 in range(FFT_CHUNK)], axis=0)
    y_ref[...] = jnp.transpose(y, (1, 0, 2)).reshape(y_ref.shape)


def _fourier_epilogue(zz, cs_ref, w_ref, norm):
    f = (_dot(zz, cs_ref[...]) * norm).astype(BF16)
    return _dot(f, w_ref[...])


def _fft2_kernel(t_ref, y_ref, cs_ref, w_ref, o_ref, zz, *, norm):
    for kk in range(FFT_CHUNK):
        z = _dot(t_ref[kk], y_ref[kk].reshape(2 * FFT_N1, FOURIER_WIDTH))
        zz[kk * FFT_N1:(kk + 1) * FFT_N1, :] = jnp.concatenate([z[:FFT_N1], z[FFT_N1:]], axis=1).astype(BF16)
    o = _fourier_epilogue(zz[...], cs_ref, w_ref, norm).astype(BF16)
    o_ref[...] = jnp.transpose(o.reshape(FFT_CHUNK, FFT_N1, FOURIER_WIDTH), (1, 0, 2))


def _fourier_latent(uf, fourier_w_bd):
    stage1, stage2, _, chan = _dft_constants()
    chunks1 = FFT_N1 // FFT_CHUNK
    y = pl.pallas_call(
        _fft1_kernel,
        grid=(BATCH, chunks1),
        in_specs=[_resident((2 * FFT_N2, FFT_N2), lambda b, j: (0, 0)),
                  pl.BlockSpec((FFT_N2, FFT_CHUNK, FOURIER_WIDTH), lambda b, j: (b, j, 0))],
        out_specs=pl.BlockSpec((FFT_N2, 2, FFT_CHUNK, FOURIER_WIDTH), lambda b, j: (b, 0, j, 0)),
        out_shape=jax.ShapeDtypeStruct((BATCH * FFT_N2, 2, FFT_N1, FOURIER_WIDTH), BF16),
        compiler_params=_params(2),
        name="fft_stage1",
    )(stage1, uf.reshape(BATCH * FFT_N2, FFT_N1, FOURIER_WIDTH))
    chunks2 = FFT_N2 // FFT_CHUNK
    out = pl.pallas_call(
        functools.partial(_fft2_kernel, norm=float((SEQ * GROUP_WIDTH) ** -0.5)),
        grid=(chunks2, BATCH),
        in_specs=[pl.BlockSpec((FFT_CHUNK, 2 * FFT_N1, 2 * FFT_N1), lambda j, b: (j, 0, 0)),
                  pl.BlockSpec((FFT_CHUNK, 2, FFT_N1, FOURIER_WIDTH), lambda j, b: (b * chunks2 + j, 0, 0, 0)),
                  _resident((2 * FOURIER_WIDTH, FOURIER_WIDTH), lambda j, b: (0, 0)),
                  _resident((FOURIER_WIDTH, FOURIER_WIDTH), lambda j, b: (0, 0))],
        out_specs=pl.BlockSpec((FFT_N1, FFT_CHUNK, FOURIER_WIDTH), lambda j, b: (b, j, 0)),
        out_shape=jax.ShapeDtypeStruct((BATCH * FFT_N1, FFT_N2, FOURIER_WIDTH), BF16),
        scratch_shapes=[pltpu.VMEM((FFT_CHUNK * FFT_N1, 2 * FOURIER_WIDTH), BF16)],
        compiler_params=_params(2),
        name="fft_stage2",
    )(stage2, y, chan, fourier_w_bd)
    return out.reshape(BATCH * SEQ, FOURIER_WIDTH)


def _fourier_ctx_kernel(m_ref, u_ref, cs_ref, w_ref, o_ref, *, norm):
    z = _dot(m_ref[...], u_ref[...])
    zz = jnp.concatenate([z[:CTX_LEN], z[CTX_LEN:]], axis=1).astype(BF16)
    o_ref[...] = _fourier_epilogue(zz, cs_ref, w_ref, norm).astype(BF16)


def _fourier_ctx(ufc, fourier_w_bd):
    _, _, ctx_dft, chan = _dft_constants()
    return pl.pallas_call(
        functools.partial(_fourier_ctx_kernel, norm=float((CTX_LEN * GROUP_WIDTH) ** -0.5)),
        grid=(BATCH,),
        in_specs=[_resident((2 * CTX_LEN, CTX_LEN), lambda b: (0, 0)),
                  pl.BlockSpec((CTX_LEN, FOURIER_WIDTH), lambda b: (b, 0)),
                  _resident((2 * FOURIER_WIDTH, FOURIER_WIDTH), lambda b: (0, 0)),
                  _resident((FOURIER_WIDTH, FOURIER_WIDTH), lambda b: (0, 0))],
        out_specs=pl.BlockSpec((CTX_LEN, FOURIER_WIDTH), lambda b: (b, 0)),
        out_shape=jax.ShapeDtypeStruct((BATCH * CTX_LEN, FOURIER_WIDTH), BF16),
        compiler_params=_params(1),
        name="fourier_ctx",
    )(ctx_dft, ufc, chan, fourier_w_bd)


def _split_pair(x, even):
    zero = jnp.zeros_like(x)
    return jnp.concatenate([jnp.where(even, x, zero), jnp.where(even, zero, x)], axis=0)


def _softmax_pv(scores, values, n_q, even):
    m = functools.reduce(jnp.maximum, [jnp.max(s, axis=1, keepdims=True) for s in scores])
    e = [jnp.exp2(s - m) for s in scores]
    l = functools.reduce(jnp.add, [jnp.sum(x, axis=1, keepdims=True) for x in e])
    o = functools.reduce(jnp.add, [_dot(x.astype(BF16), v) for x, v in zip(e, values)])
    o = o / l
    return jnp.where(even, o[:n_q], o[n_q:])


ATTN_TILE_ROWS = 32
ATTN_TILE = ATTN_TILE_ROWS * GRID_W
ATTN_HALO = WIN_R * GRID_W
SOFTMAX_ROWS = 32


def _kv_window_row(t):
    return jnp.clip(t * ATTN_TILE_ROWS - WIN_R, 0, GRID_ROWS - ATTN_TILE_ROWS - 2 * WIN_R)


def _nat_kernel(q_ref, kbuf, vbuf, kx_ref, vx_ref, bias_ref, o_ref, sbuf_a, sbuf_b, pbuf):
    t = pl.program_id(1)
    even = lax.broadcasted_iota(jnp.int32, (GRID_W, 128), 1) < HEAD_DIM
    n_win = WIN_R * GRID_W

    def window_offset(j):
        r = t * ATTN_TILE_ROWS + j
        rs = jnp.clip(r - WIN_R // 2, 0, GRID_ROWS - WIN_R)
        return pl.multiple_of((rs - _kv_window_row(t)) * GRID_W, GRID_W)

    def query_offset(j):
        return j * GRID_W if isinstance(j, int) else pl.multiple_of(j * GRID_W, GRID_W)

    def scores(j, sbuf):
        r = t * ATTN_TILE_ROWS + j
        off = window_offset(j)
        dr0 = jnp.clip(r - WIN_R // 2, 0, GRID_ROWS - WIN_R) - r + WIN_R - 1
        qoff = query_offset(j)
        for p in range(N_PAIRS):
            lanes = slice(128 * p, 128 * (p + 1))
            qs = _split_pair(q_ref[pl.ds(qoff, GRID_W), lanes], even)
            bias = jnp.concatenate([bias_ref[p, dr0 + 2 * m] for m in range(WIN_R // 2)], axis=1)
            sbuf[p, :, 0:n_win] = _dot_nt(qs, kbuf[pl.ds(off, n_win), lanes]) + bias
            sbuf[p, :, n_win:] = _dot_nt(qs, kx_ref[:, lanes])

    def outputs(j, sbuf):
        off = window_offset(j)
        qoff = query_offset(j)
        for p in range(N_PAIRS):
            lanes = slice(128 * p, 128 * (p + 1))
            sums = []
            for g in range(2 * GRID_W // SOFTMAX_ROWS):
                rows = slice(g * SOFTMAX_ROWS, (g + 1) * SOFTMAX_ROWS)
                s = sbuf[p, rows, :]
                e = jnp.exp2(s - jnp.max(s, axis=1, keepdims=True))
                sums.append(jnp.sum(e, axis=1, keepdims=True))
                pbuf[p, rows, :] = e.astype(BF16)
            l = jnp.concatenate(sums, axis=0)
            o = (_dot(pbuf[p, :, :n_win], vbuf[pl.ds(off, n_win), lanes])
                 + _dot(pbuf[p, :, n_win:], vx_ref[:, lanes]))
            o = o / l
            o_ref[pl.ds(qoff, GRID_W), lanes] = jnp.where(even, o[:GRID_W], o[GRID_W:]).astype(BF16)

    scores(0, sbuf_a)

    def body(i, carry):
        scores(2 * i + 1, sbuf_b)
        outputs(2 * i, sbuf_a)
        scores(jnp.minimum(2 * i + 2, ATTN_TILE_ROWS - 1), sbuf_a)
        outputs(2 * i + 1, sbuf_b)
        return carry

    lax.fori_loop(0, ATTN_TILE_ROWS // 2, body, 0)


def _neighbourhood_attention(qkv, qkv_ctx, bias):
    tiles = SEQ // ATTN_TILE
    cur = lambda col: (lambda b, t: (b * tiles + t, col))
    blk = lambda f: pl.BlockSpec((ATTN_TILE, ATTN_WIDTH), f)
    kv = lambda col: pl.BlockSpec(
        (pl.Element(ATTN_TILE + 2 * ATTN_HALO), pl.Element(ATTN_WIDTH)),
        lambda b, t: (pl.multiple_of(b * SEQ + _kv_window_row(t) * GRID_W, GRID_W), col * ATTN_WIDTH))
    return pl.pallas_call(
        _nat_kernel,
        grid=(BATCH, tiles),
        in_specs=[blk(cur(0)), kv(1), kv(2),
                  pl.BlockSpec((CTX_LEN, ATTN_WIDTH), lambda b, t: (b, 1)),
                  pl.BlockSpec((CTX_LEN, ATTN_WIDTH), lambda b, t: (b, 2)),
                  _resident(bias.shape, lambda b, t: (0, 0, 0, 0))],
        out_specs=blk(cur(0)),
        out_shape=jax.ShapeDtypeStruct((BATCH * SEQ, ATTN_WIDTH), BF16),
        scratch_shapes=[pltpu.VMEM((N_PAIRS, 2 * GRID_W, WIN_R * GRID_W + CTX_LEN), F32),
                        pltpu.VMEM((N_PAIRS, 2 * GRID_W, WIN_R * GRID_W + CTX_LEN), F32),
                        pltpu.VMEM((N_PAIRS, 2 * GRID_W, WIN_R * GRID_W + CTX_LEN), BF16)],
        compiler_params=_params(2),
        name="nat_attention",
    )(qkv, qkv, qkv, qkv_ctx, qkv_ctx, bias)


def _ctx_attn_kernel(q_ref, k_ref, v_ref, o_ref):
    even = lax.broadcasted_iota(jnp.int32, (CTX_LEN, 128), 1) < HEAD_DIM
    for p in range(N_PAIRS):
        lanes = slice(128 * p, 128 * (p + 1))
        qs = _split_pair(q_ref[:, lanes], even)
        s = _dot_nt(qs, k_ref[:, lanes])
        o_ref[:, lanes] = _softmax_pv([s], [v_ref[:, lanes]], CTX_LEN, even).astype(BF16)


def _context_attention(qkv_ctx):
    blk = lambda col: pl.BlockSpec((CTX_LEN, ATTN_WIDTH), lambda b: (b, col))
    return pl.pallas_call(
        _ctx_attn_kernel,
        grid=(BATCH,),
        in_specs=[blk(0), blk(1), blk(2)],
        out_specs=blk(0),
        out_shape=jax.ShapeDtypeStruct((BATCH * CTX_LEN, ATTN_WIDTH), BF16),
        compiler_params=_params(1),
        name="ctx_attention",
    )(qkv_ctx, qkv_ctx, qkv_ctx)


V7X_MXU_WIDTH = 256
FFN_CHUNKS = ((0, 6 * V7X_MXU_WIDTH), (6 * V7X_MXU_WIDTH, D_FF))


def _outffn_kernel(x_ref, yp_ref, yf_ref, at_ref, mod_ref, wo_ref, g2_ref, w1_ref, w3_ref, w2_ref,
                   *rest, final):
    o_ref = rest[-1]
    y = jnp.concatenate([yp_ref[...], yf_ref[...], at_ref[...]], axis=1)
    x1 = x_ref[...] + mod_ref[0, 2:3, :] * _dot(y, wo_ref[...])
    h = (_rms(x1) * g2_ref[...] * (1.0 + mod_ref[0, 4:5, :]) + mod_ref[0, 3:4, :]).astype(BF16)
    ffn = jnp.zeros_like(x1)
    for lo, hi in FFN_CHUNKS:
        cols = slice(lo, hi)
        a = _dot(h, w1_ref[:, cols])
        gated = (a * jax.nn.sigmoid(a) * _dot(h, w3_ref[:, cols])).astype(BF16)
        ffn = ffn + _dot(gated, w2_ref[cols, :])
    x2 = x1 + mod_ref[0, 5:6, :] * ffn
    if final:
        x2 = _rms(x2) * rest[0][...]
    o_ref[...] = x2


def _outffn(x, yp, yf, at, mod, w_out, g2, w1, w3, w2, final_g, layer, mod_row, tm):
    n = x.shape[0]
    row = lambda w: pl.BlockSpec((tm, w), lambda i: (i, 0))
    in_specs = [row(D_MODEL), row(256), row(256), row(512),
                pl.BlockSpec((1, 6, D_MODEL), lambda i: (mod_row(i, tm), 0, 0)),
                _layer_weight(layer, D_MODEL, D_MODEL),
                _resident((1, D_MODEL), lambda i: (0, 0)),
                _layer_weight(layer, D_MODEL, D_FF),
                _layer_weight(layer, D_MODEL, D_FF),
                _layer_weight(layer, D_FF, D_MODEL)]
    args = [x, yp, yf, at, mod, w_out, g2, w1, w3, w2]
    if final_g is not None:
        in_specs.append(_resident((1, D_MODEL), lambda i: (0, 0)))
        args.append(final_g)
    return pl.pallas_call(
        functools.partial(_outffn_kernel, final=final_g is not None),
        grid=(n // tm,),
        in_specs=in_specs,
        out_specs=row(D_MODEL),
        out_shape=jax.ShapeDtypeStruct((n, D_MODEL), F32),
        compiler_params=_params(1),
        name="outproj_ffn",
    )(*args)


def _latent_mod_row(i, tm):
    return (i * tm) // SEQ


def _ctx_mod_row(i, tm):
    return BATCH


def kernel(x, c, ctx, c_ctx, w_ada, b_ada, norm1_g, w_in, pool_w, pool_scale, fourier_w, nat_bias, w_out,
           norm2_g, w_ffn1, w_ffn3, w_ffn2, final_g):
    cond = jnp.concatenate([c, c_ctx[None, :], jnp.zeros((8 - BATCH - 1, D_MODEL), F32)], axis=0)
    mods = _ada_mod(cond, w_ada, b_ada)
    x = x.reshape(BATCH * SEQ, D_MODEL)
    ctx = ctx.reshape(BATCH * CTX_LEN, D_MODEL)
    w_in, w_out, w_ffn1, w_ffn3, w_ffn2 = (w.astype(BF16) for w in (w_in, w_out, w_ffn1, w_ffn3, w_ffn2))
    for l in range(DEPTH):
        last = l == DEPTH - 1
        mod = mods[l]
        g1 = norm1_g[l][None, :]
        g2 = norm2_g[l][None, :]
        pool_w_bd = _group_block_diag(pool_w[l])
        fourier_w_bd = _group_block_diag(fourier_w[l])
        scale = pool_scale[l][None, :]
        ffn_w = (w_out, g2, w_ffn1, w_ffn3, w_ffn2)

        upc, ufc, qkvc = _inproj(ctx, mod, g1, w_in, l, _ctx_mod_row, 256)
        up, uf, qkv = _inproj(x, mod, g1, w_in, l, _latent_mod_row, 2048)
        yp = _pool(up, pool_w_bd, scale, SEQ, 1024)
        yf = _fourier_latent(uf, fourier_w_bd)
        at = _neighbourhood_attention(qkv, qkvc, _attention_bias(nat_bias[l]))
        x = _outffn(x, yp, yf, at, mod, *ffn_w, final_g[None, :] if last else None, l, _latent_mod_row, 1024)
        if not last:
            ypc = _pool(upc, pool_w_bd, scale, CTX_LEN, CTX_LEN)
            yfc = _fourier_ctx(ufc, fourier_w_bd)
            atc = _context_attention(qkvc)
            ctx = _outffn(ctx, ypc, yfc, atc, mod, *ffn_w, None, l, _ctx_mod_row, 256)
    return x.reshape(BATCH, SEQ, D_MODEL)
```

```python
import functools

import numpy as np
import jax
import jax.numpy as jnp
from jax import lax
from jax.experimental import pallas as pl
from jax.experimental.pallas import tpu as pltpu

D_MODEL = 1024
BATCH = 4
SEQ = 8192
DEPTH = 4
GRID_W = 64
GRID_ROWS = SEQ // GRID_W
CTX_LEN = 256
POOL_WINDOWS = (2, 4, 8, 16)
POOL_WIDTH = 256
FOURIER_WIDTH = 256
GROUP_WIDTH = 64
HEAD_DIM = 64
N_HEADS = 8
N_PAIRS = N_HEADS // 2
ATTN_WIDTH = 512
WIN_R = 8
WIN_C = 16
D_FF = 2816
EPS = 1e-6
NEG_INF = -1e30
LOG2_E = 1.4426950408889634
Q_SCALE = HEAD_DIM ** -0.5 * LOG2_E

FFT_N1 = 128
FFT_N2 = 64
POOL_HALO = 16
BIAS_SHAPE = (N_PAIRS, 2 * WIN_R - 2, 2 * GRID_W, 2 * GRID_W)

V7X_VMEM_LIMIT = 56 * 1024 * 1024

F32 = jnp.float32
BF16 = jnp.bfloat16


def _dot(a, b):
    return jnp.dot(a, b, preferred_element_type=F32)


def _dot_nt(a, b):
    return lax.dot_general(a, b, (((1,), (1,)), ((), ())), preferred_element_type=F32)


def _params(n_grid_dims):
    return pltpu.CompilerParams(dimension_semantics=("arbitrary",) * n_grid_dims,
                                vmem_limit_bytes=V7X_VMEM_LIMIT)


def _resident(block_shape, index_map):
    return pl.BlockSpec(block_shape, index_map, pipeline_mode=pl.Buffered(1))


def _block_diag(m, groups):
    w = m.shape[0]
    out = np.zeros((groups * w, groups * w), m.dtype)
    for g in range(groups):
        out[g * w:(g + 1) * w, g * w:(g + 1) * w] = m
    return out


@functools.lru_cache(maxsize=None)
def _dft_constants():
    two_pi = 2.0 * np.pi
    a2 = two_pi * (np.outer(np.arange(FFT_N2), np.arange(FFT_N2)) % FFT_N2) / FFT_N2
    stage1 = np.stack([np.cos(a2), -np.sin(a2)], axis=1).reshape(2 * FFT_N2, FFT_N2)
    k = FFT_N2 * np.arange(FFT_N1)[None, :, None] + np.arange(FFT_N2)[:, None, None]
    n1 = np.arange(FFT_N1)[None, None, :]
    ang = two_pi * ((k * n1) % SEQ) / SEQ
    mr, mi = np.cos(ang), -np.sin(ang)
    stage2 = np.concatenate([np.concatenate([mr, -mi], axis=2),
                             np.concatenate([mi, mr], axis=2)], axis=1)
    ac = two_pi * (np.outer(np.arange(CTX_LEN), np.arange(CTX_LEN)) % CTX_LEN) / CTX_LEN
    ctx_dft = np.concatenate([np.cos(ac), -np.sin(ac)], axis=0)
    ag = two_pi * (np.outer(np.arange(GROUP_WIDTH), np.arange(GROUP_WIDTH)) % GROUP_WIDTH) / GROUP_WIDTH
    chan = np.concatenate([_block_diag(np.cos(ag), 4), _block_diag(np.sin(ag), 4)], axis=0)
    as_bf16 = lambda a: jnp.asarray(a, dtype=F32).astype(BF16)
    return as_bf16(stage1), as_bf16(stage2), as_bf16(ctx_dft), as_bf16(chan)


def _group_block_diag(w):
    out = jnp.zeros((4, GROUP_WIDTH, 4, GROUP_WIDTH), w.dtype)
    for g in range(4):
        out = out.at[g, :, g, :].set(w[g])
    return out.reshape(4 * GROUP_WIDTH, 4 * GROUP_WIDTH).astype(BF16)


def _attention_bias(table):
    col = np.arange(GRID_W)
    cs = np.clip(col - WIN_C // 2, 0, GRID_W - WIN_C)
    col_mask = (col[None, :] >= cs[:, None]) & (col[None, :] < cs[:, None] + WIN_C)
    dc_idx = np.clip(col[None, :] - col[:, None] + WIN_C - 1, 0, 2 * WIN_C - 2)
    onehot = (dc_idx[None] == np.arange(2 * WIN_C - 1)[:, None, None]).astype(np.float32)
    bt = jnp.einsum('hrd,dcj->hrcj', table.astype(F32), jnp.asarray(onehot),
                    precision=lax.Precision.HIGHEST)
    bt = jnp.where(col_mask[None, None], bt * LOG2_E, NEG_INF)
    b = jnp.concatenate([bt[:, :-1], bt[:, 1:]], axis=-1)
    b = b.reshape(N_PAIRS, 2, 2 * WIN_R - 2, GRID_W, 2 * GRID_W).transpose(0, 2, 1, 3, 4)
    return b.reshape(BIAS_SHAPE)


def _ada_kernel(c_ref, w_ref, b_ref, o_ref):
    c = c_ref[...]
    a = (c * jax.nn.sigmoid(c)).astype(BF16)
    o_ref[0] = _dot(a, w_ref[0].astype(BF16)) + b_ref[0]


def _ada_mod(cond, w_ada, b_ada):
    tn = 1536
    out = pl.pallas_call(
        _ada_kernel,
        grid=(DEPTH, 6 * D_MODEL // tn),
        in_specs=[_resident((8, D_MODEL), lambda l, j: (0, 0)),
                  pl.BlockSpec((1, D_MODEL, tn), lambda l, j: (l, 0, j)),
                  pl.BlockSpec((1, 1, tn), lambda l, j: (l, 0, j))],
        out_specs=pl.BlockSpec((1, 8, tn), lambda l, j: (l, 0, j)),
        out_shape=jax.ShapeDtypeStruct((DEPTH, 8, 6 * D_MODEL), F32),
        compiler_params=_params(2),
        name="ada_mod",
    )(cond, w_ada, b_ada.reshape(DEPTH, 1, 6 * D_MODEL))
    return out.reshape(DEPTH, 8, 6, D_MODEL)


def _rms(x):
    return x * lax.rsqrt(jnp.mean(x * x, axis=-1, keepdims=True) + EPS)


def _inproj_kernel(x_ref, mod_ref, g_ref, w_ref, up_ref, uf_ref, qkv_ref):
    y = _rms(x_ref[...]) * g_ref[...]
    h = (y * (1.0 + mod_ref[0, 1:2, :]) + mod_ref[0, 0:1, :]).astype(BF16)
    up_ref[...] = _dot(h, w_ref[:, 0:256]).astype(BF16)
    uf_ref[...] = _dot(h, w_ref[:, 256:512]).astype(BF16)
    qkv_ref[:, 0:512] = (_dot(h, w_ref[:, 512:1024]) * Q_SCALE).astype(BF16)
    qkv_ref[:, 512:1024] = _dot(h, w_ref[:, 1024:1536]).astype(BF16)
    qkv_ref[:, 1024:1536] = _dot(h, w_ref[:, 1536:2048]).astype(BF16)


def _layer_weight(layer, rows, cols):
    return _resident((None, rows, cols), lambda i: (layer, 0, 0))


def _inproj(x, mod, g, w_in, layer, mod_row, tm):
    n = x.shape[0]
    return pl.pallas_call(
        _inproj_kernel,
        grid=(n // tm,),
        in_specs=[pl.BlockSpec((tm, D_MODEL), lambda i: (i, 0)),
                  pl.BlockSpec((1, 6, D_MODEL), lambda i: (mod_row(i, tm), 0, 0)),
                  _resident((1, D_MODEL), lambda i: (0, 0)),
                  _layer_weight(layer, D_MODEL, 2048)],
        out_specs=[pl.BlockSpec((tm, 256), lambda i: (i, 0)),
                   pl.BlockSpec((tm, 256), lambda i: (i, 0)),
                   pl.BlockSpec((tm, 1536), lambda i: (i, 0))],
        out_shape=[jax.ShapeDtypeStruct((n, 256), BF16),
                   jax.ShapeDtypeStruct((n, 256), BF16),
                   jax.ShapeDtypeStruct((n, 1536), BF16)],
        compiler_params=_params(1),
        name="inproj",
    )(x, mod, g, w_in)


COL_BLOCK = 16
N_COL_BLOCKS = GRID_W // COL_BLOCK
BLOCK_TOKENS = GRID_ROWS * COL_BLOCK


def _inproj_grid_kernel(x_ref, mod_ref, g_ref, w_ref, up_ref, uf_ref, q_ref, k_ref, v_ref):
    y = _rms(x_ref[...]) * g_ref[...]
    h = (y * (1.0 + mod_ref[0, 1:2, :]) + mod_ref[0, 0:1, :]).astype(BF16)
    up_ref[...] = _dot(h, w_ref[:, 0:256]).astype(BF16)
    uf_ref[...] = _dot(h, w_ref[:, 256:512]).astype(BF16)
    q_ref[...] = (_dot(h, w_ref[:, 512:1024]) * Q_SCALE).astype(BF16)
    for dst, lo in ((k_ref, 1024), (v_ref, 1536)):
        kv = _dot(h, w_ref[:, lo:lo + ATTN_WIDTH]).astype(BF16)
        for r in range(kv.shape[0] // GRID_W):
            for t in range(N_COL_BLOCKS):
                src = r * GRID_W + t * COL_BLOCK
                dst[t, r * COL_BLOCK:(r + 1) * COL_BLOCK, :] = kv[src:src + COL_BLOCK, :]


def _inproj_grid(x, mod, g, w_in, layer, tm):
    n = x.shape[0]
    tiles = SEQ // tm
    kv_rows = tm // N_COL_BLOCKS
    kv_spec = pl.BlockSpec((N_COL_BLOCKS, kv_rows, ATTN_WIDTH), lambda i: (i // tiles, i % tiles, 0))
    kv_shape = jax.ShapeDtypeStruct((BATCH * N_COL_BLOCKS, BLOCK_TOKENS, ATTN_WIDTH), BF16)
    return pl.pallas_call(
        _inproj_grid_kernel,
        grid=(n // tm,),
        in_specs=[pl.BlockSpec((tm, D_MODEL), lambda i: (i, 0)),
                  pl.BlockSpec((1, 6, D_MODEL), lambda i: (_latent_mod_row(i, tm), 0, 0)),
                  _resident((1, D_MODEL), lambda i: (0, 0)),
                  _layer_weight(layer, D_MODEL, 2048)],
        out_specs=[pl.BlockSpec((tm, 256), lambda i: (i, 0)),
                   pl.BlockSpec((tm, 256), lambda i: (i, 0)),
                   pl.BlockSpec((tm, ATTN_WIDTH), lambda i: (i, 0)),
                   kv_spec, kv_spec],
        out_shape=[jax.ShapeDtypeStruct((n, 256), BF16),
                   jax.ShapeDtypeStruct((n, 256), BF16),
                   jax.ShapeDtypeStruct((n, ATTN_WIDTH), BF16),
                   kv_shape, kv_shape],
        compiler_params=_params(1),
        name="inproj_grid",
    )(x, mod, g, w_in)


POOL_SUB = 256
POOL_BAND_K = POOL_SUB + 2 * POOL_HALO


@functools.lru_cache(maxsize=None)
def _pool_bands():
    t = np.arange(POOL_SUB)[:, None]
    d = np.arange(POOL_BAND_K)[None, :] - POOL_HALO - t
    member = [((d >= -(w // 2)) & (d < w - w // 2)).astype(np.float64) for w in POOL_WINDOWS]
    centred = [m / w - (d == 0) for m, w in zip(member, POOL_WINDOWS)]
    interior = np.stack([np.concatenate(centred[0:2], axis=1), np.concatenate(centred[2:4], axis=1)], axis=0)
    as_bf16 = lambda a: jnp.asarray(a, dtype=F32).astype(BF16)
    return as_bf16(np.concatenate(member, axis=0)), as_bf16(interior)


def _pool_kernel(band_ref, inner_ref, prev_ref, cur_ref, next_ref, pw_ref, ps_ref, o_ref, *, seq_tiles, nsub):
    st = pl.program_id(0) % seq_tiles
    ub_all = jnp.concatenate([prev_ref[...], cur_ref[...], next_ref[...]], axis=0)
    left = lax.broadcasted_iota(jnp.int32, (POOL_BAND_K, 128), 1) < GROUP_WIDTH

    for s in range(nsub):
        ub = ub_all[s * POOL_SUB:s * POOL_SUB + POOL_BAND_K]
        cur = ub[POOL_HALO:POOL_HALO + POOL_SUB]

        def interior(ub=ub):
            halves = []
            for h in range(2):
                t = ub[:, 128 * h:128 * (h + 1)]
                zero = jnp.zeros_like(t)
                stacked = jnp.concatenate([jnp.where(left, t, zero), jnp.where(left, zero, t)], axis=0)
                halves.append(_dot(inner_ref[h], stacked))
            return jnp.concatenate(halves, axis=1).astype(BF16)

        def clipped(ub=ub, cur=cur, s=s):
            lo_b = jnp.where(st == 0, 0, -POOL_HALO) if s == 0 else -POOL_HALO
            hi_b = (jnp.where(st == seq_tiles - 1, POOL_SUB, POOL_SUB + POOL_HALO) if s == nsub - 1
                    else POOL_SUB + POOL_HALO)
            p = lax.broadcasted_iota(jnp.int32, ub.shape, 0) - POOL_HALO
            ubv = jnp.where((p >= lo_b) & (p < hi_b), ub, jnp.zeros_like(ub))
            sums = _dot(band_ref[...], ubv)
            tcol = lax.broadcasted_iota(jnp.int32, (POOL_SUB, 1), 0)
            group = lax.broadcasted_iota(jnp.int32, (POOL_SUB, POOL_WIDTH), 1) // GROUP_WIDTH
            mean = jnp.zeros((POOL_SUB, POOL_WIDTH), F32)
            for g, w in enumerate(POOL_WINDOWS):
                cnt = jnp.minimum(tcol + (w - w // 2), hi_b) - jnp.maximum(tcol - w // 2, lo_b)
                mean = jnp.where(group == g, sums[g * POOL_SUB:(g + 1) * POOL_SUB] / cnt.astype(F32), mean)
            return (mean - cur.astype(F32)).astype(BF16)

        at_start = (st == 0) if s == 0 else None
        at_end = (st == seq_tiles - 1) if s == nsub - 1 else None
        if at_start is None and at_end is None:
            diff = interior()
        else:
            edge = at_start if at_end is None else (at_end if at_start is None else at_start | at_end)
            diff = lax.cond(edge, clipped, interior)
        o_ref[s * POOL_SUB:(s + 1) * POOL_SUB] = (_dot(diff, pw_ref[...]) * ps_ref[...]).astype(BF16)


def _pool(up, pool_w_bd, pool_scale, seq_len, tm):
    n = up.shape[0]
    hb = tm // POOL_HALO
    n_halo_blocks = n // POOL_HALO
    bands, inner = _pool_bands()
    return pl.pallas_call(
        functools.partial(_pool_kernel, seq_tiles=seq_len // tm, nsub=tm // POOL_SUB),
        grid=(n // tm,),
        in_specs=[_resident(bands.shape, lambda i: (0, 0)),
                  _resident(inner.shape, lambda i: (0, 0, 0)),
                  pl.BlockSpec((POOL_HALO, POOL_WIDTH), lambda i: (jnp.maximum(i * hb - 1, 0), 0)),
                  pl.BlockSpec((tm, POOL_WIDTH), lambda i: (i, 0)),
                  pl.BlockSpec((POOL_HALO, POOL_WIDTH),
                               lambda i: (jnp.minimum((i + 1) * hb, n_halo_blocks - 1), 0)),
                  _resident((POOL_WIDTH, POOL_WIDTH), lambda i: (0, 0)),
                  _resident((1, POOL_WIDTH), lambda i: (0, 0))],
        out_specs=pl.BlockSpec((tm, POOL_WIDTH), lambda i: (i, 0)),
        out_shape=jax.ShapeDtypeStruct((n, POOL_WIDTH), BF16),
        compiler_params=_params(1),
        name="pool",
    )(bands, inner, up, up, up, pool_w_bd, pool_scale)


FFT_CHUNK = 32


def _fft1_kernel(m_ref, x_ref, y_ref):
    xt = jnp.transpose(x_ref[...], (1, 0, 2))
    y = jnp.stack([_dot(m_ref[...], xt[n1]).astype(BF16) for n1 in range(FFT_CHUNK)], axis=0)
    y_ref[...] = jnp.transpose(y, (1, 0, 2)).reshape(y_ref.shape)


def _fourier_epilogue(zz, cs_ref, w_ref, norm):
    f = (_dot(zz, cs_ref[...]) * norm).astype(BF16)
    return _dot(f, w_ref[...])


def _fft2_kernel(t_ref, y_ref, cs_ref, w_ref, o_ref, zz, *, norm):
    for kk in range(FFT_CHUNK):
        z = _dot(t_ref[kk], y_ref[kk].reshape(2 * FFT_N1, FOURIER_WIDTH))
        zz[kk * FFT_N1:(kk + 1) * FFT_N1, :] = jnp.concatenate([z[:FFT_N1], z[FFT_N1:]], axis=1).astype(BF16)
    o = _fourier_epilogue(zz[...], cs_ref, w_ref, norm).astype(BF16)
    o_ref[...] = jnp.transpose(o.reshape(FFT_CHUNK, FFT_N1, FOURIER_WIDTH), (1, 0, 2))


def _fourier_latent(uf, fourier_w_bd):
    stage1, stage2, _, chan = _dft_constants()
    chunks1 = FFT_N1 // FFT_CHUNK
    y = pl.pallas_call(
        _fft1_kernel,
        grid=(BATCH, chunks1),
        in_specs=[_resident((2 * FFT_N2, FFT_N2), lambda b, j: (0, 0)),
                  pl.BlockSpec((FFT_N2, FFT_CHUNK, FOURIER_WIDTH), lambda b, j: (b, j, 0))],
        out_specs=pl.BlockSpec((FFT_N2, 2, FFT_CHUNK, FOURIER_WIDTH), lambda b, j: (b, 0, j, 0)),
        out_shape=jax.ShapeDtypeStruct((BATCH * FFT_N2, 2, FFT_N1, FOURIER_WIDTH), BF16),
        compiler_params=_params(2),
        name="fft_stage1",
    )(stage1, uf.reshape(BATCH * FFT_N2, FFT_N1, FOURIER_WIDTH))
    chunks2 = FFT_N2 // FFT_CHUNK
    out = pl.pallas_call(
        functools.partial(_fft2_kernel, norm=float((SEQ * GROUP_WIDTH) ** -0.5)),
        grid=(chunks2, BATCH),
        in_specs=[pl.BlockSpec((FFT_CHUNK, 2 * FFT_N1, 2 * FFT_N1), lambda j, b: (j, 0, 0)),
                  pl.BlockSpec((FFT_CHUNK, 2, FFT_N1, FOURIER_WIDTH), lambda j, b: (b * chunks2 + j, 0, 0, 0)),
                  _resident((2 * FOURIER_WIDTH, FOURIER_WIDTH), lambda j, b: (0, 0)),
                  _resident((FOURIER_WIDTH, FOURIER_WIDTH), lambda j, b: (0, 0))],
        out_specs=pl.BlockSpec((FFT_N1, FFT_CHUNK, FOURIER_WIDTH), lambda j, b: (b, j, 0)),
        out_shape=jax.ShapeDtypeStruct((BATCH * FFT_N1, FFT_N2, FOURIER_WIDTH), BF16),
        scratch_shapes=[pltpu.VMEM((FFT_CHUNK * FFT_N1, 2 * FOURIER_WIDTH), BF16)],
        compiler_params=_params(2),
        name="fft_stage2",
    )(stage2, y, chan, fourier_w_bd)
    return out.reshape(BATCH * SEQ, FOURIER_WIDTH)


def _fourier_ctx_kernel(m_ref, u_ref, cs_ref, w_ref, o_ref, *, norm):
    z = _dot(m_ref[...], u_ref[...])
    zz = jnp.concatenate([z[:CTX_LEN], z[CTX_LEN:]], axis=1).astype(BF16)
    o_ref[...] = _fourier_epilogue(zz, cs_ref, w_ref, norm).astype(BF16)


def _fourier_ctx(ufc, fourier_w_bd):
    _, _, ctx_dft, chan = _dft_constants()
    return pl.pallas_call(
        functools.partial(_fourier_ctx_kernel, norm=float((CTX_LEN * GROUP_WIDTH) ** -0.5)),
        grid=(BATCH,),
        in_specs=[_resident((2 * CTX_LEN, CTX_LEN), lambda b: (0, 0)),
                  pl.BlockSpec((CTX_LEN, FOURIER_WIDTH), lambda b: (b, 0)),
                  _resident((2 * FOURIER_WIDTH, FOURIER_WIDTH), lambda b: (0, 0)),
                  _resident((FOURIER_WIDTH, FOURIER_WIDTH), lambda b: (0, 0))],
        out_specs=pl.BlockSpec((CTX_LEN, FOURIER_WIDTH), lambda b: (b, 0)),
        out_shape=jax.ShapeDtypeStruct((BATCH * CTX_LEN, FOURIER_WIDTH), BF16),
        compiler_params=_params(1),
        name="fourier_ctx",
    )(ctx_dft, ufc, chan, fourier_w_bd)


def _split_pair(x, even):
    zero = jnp.zeros_like(x)
    return jnp.concatenate([jnp.where(even, x, zero), jnp.where(even, zero, x)], axis=0)


def _softmax_pv(scores, values, n_q, even):
    m = functools.reduce(jnp.maximum, [jnp.max(s, axis=1, keepdims=True) for s in scores])
    e = [jnp.exp2(s - m) for s in scores]
    l = functools.reduce(jnp.add, [jnp.sum(x, axis=1, keepdims=True) for x in e])
    o = functools.reduce(jnp.add, [_dot(x.astype(BF16), v) for x, v in zip(e, values)])
    o = o / l
    return jnp.where(even, o[:n_q], o[n_q:])


ATTN_TILE_ROWS = 32
ATTN_TILE = ATTN_TILE_ROWS * GRID_W
ATTN_HALO = WIN_R * GRID_W
ATTN_ROWS_PER_TRIP = 4
SOFTMAX_ROWS = 32


def _kv_window_row(t):
    return jnp.clip(t * ATTN_TILE_ROWS - WIN_R, 0, GRID_ROWS - ATTN_TILE_ROWS - 2 * WIN_R)


def _nat_kernel(q_ref, kbuf, vbuf, kx_ref, vx_ref, bias_ref, o_ref, sbuf_a, sbuf_b, pbuf):
    t = pl.program_id(1)
    even = lax.broadcasted_iota(jnp.int32, (GRID_W, 128), 1) < HEAD_DIM
    n_win = WIN_R * GRID_W

    def window_offset(j):
        r = t * ATTN_TILE_ROWS + j
        rs = jnp.clip(r - WIN_R // 2, 0, GRID_ROWS - WIN_R)
        return pl.multiple_of((rs - _kv_window_row(t)) * GRID_W, GRID_W)

    def query_offset(j):
        return j * GRID_W if isinstance(j, int) else pl.multiple_of(j * GRID_W, GRID_W)

    def scores(j, sbuf):
        r = t * ATTN_TILE_ROWS + j
        off = window_offset(j)
        dr0 = jnp.clip(r - WIN_R // 2, 0, GRID_ROWS - WIN_R) - r + WIN_R - 1
        qoff = query_offset(j)
        for p in range(N_PAIRS):
            lanes = slice(128 * p, 128 * (p + 1))
            qs = _split_pair(q_ref[pl.ds(qoff, GRID_W), lanes], even)
            bias = jnp.concatenate([bias_ref[p, dr0 + 2 * m] for m in range(WIN_R // 2)], axis=1)
            sbuf[p, :, 0:n_win] = _dot_nt(qs, kbuf[pl.ds(off, n_win), lanes]) + bias
            sbuf[p, :, n_win:] = _dot_nt(qs, kx_ref[:, lanes])

    def outputs(j, sbuf):
        off = window_offset(j)
        qoff = query_offset(j)
        for p in range(N_PAIRS):
            lanes = slice(128 * p, 128 * (p + 1))
            sums = []
            for g in range(2 * GRID_W // SOFTMAX_ROWS):
                rows = slice(g * SOFTMAX_ROWS, (g + 1) * SOFTMAX_ROWS)
                s = sbuf[p, rows, :]
                e = jnp.exp2(s - jnp.max(s, axis=1, keepdims=True))
                sums.append(jnp.sum(e, axis=1, keepdims=True))
                pbuf[p, rows, :] = e.astype(BF16)
            l = jnp.concatenate(sums, axis=0)
            o = (_dot(pbuf[p, :, :n_win], vbuf[pl.ds(off, n_win), lanes])
                 + _dot(pbuf[p, :, n_win:], vx_ref[:, lanes]))
            o = o / l
            o_ref[pl.ds(qoff, GRID_W), lanes] = jnp.where(even, o[:GRID_W], o[GRID_W:]).astype(BF16)

    scores(0, sbuf_a)

    def body(i, carry):
        base = ATTN_ROWS_PER_TRIP * i
        for k in range(ATTN_ROWS_PER_TRIP):
            cur, nxt = (sbuf_a, sbuf_b) if k % 2 == 0 else (sbuf_b, sbuf_a)
            scores(jnp.minimum(base + k + 1, ATTN_TILE_ROWS - 1), nxt)
            outputs(base + k, cur)
        return carry

    lax.fori_loop(0, ATTN_TILE_ROWS // ATTN_ROWS_PER_TRIP, body, 0)


def _neighbourhood_attention(qkv, qkv_ctx, bias):
    tiles = SEQ // ATTN_TILE
    cur = lambda col: (lambda b, t: (b * tiles + t, col))
    blk = lambda f: pl.BlockSpec((ATTN_TILE, ATTN_WIDTH), f)
    kv = lambda col: pl.BlockSpec(
        (pl.Element(ATTN_TILE + 2 * ATTN_HALO), pl.Element(ATTN_WIDTH)),
        lambda b, t: (pl.multiple_of(b * SEQ + _kv_window_row(t) * GRID_W, GRID_W), col * ATTN_WIDTH))
    return pl.pallas_call(
        _nat_kernel,
        grid=(BATCH, tiles),
        in_specs=[blk(cur(0)), kv(1), kv(2),
                  pl.BlockSpec((CTX_LEN, ATTN_WIDTH), lambda b, t: (b, 1)),
                  pl.BlockSpec((CTX_LEN, ATTN_WIDTH), lambda b, t: (b, 2)),
                  _resident(bias.shape, lambda b, t: (0, 0, 0, 0))],
        out_specs=blk(cur(0)),
        out_shape=jax.ShapeDtypeStruct((BATCH * SEQ, ATTN_WIDTH), BF16),
        scratch_shapes=[pltpu.VMEM((N_PAIRS, 2 * GRID_W, WIN_R * GRID_W + CTX_LEN), F32),
                        pltpu.VMEM((N_PAIRS, 2 * GRID_W, WIN_R * GRID_W + CTX_LEN), F32),
                        pltpu.VMEM((N_PAIRS, 2 * GRID_W, WIN_R * GRID_W + CTX_LEN), BF16)],
        compiler_params=_params(2),
        name="nat_attention",
    )(qkv, qkv, qkv, qkv_ctx, qkv_ctx, bias)


def _ctx_attn_kernel(q_ref, k_ref, v_ref, o_ref):
    even = lax.broadcasted_iota(jnp.int32, (CTX_LEN, 128), 1) < HEAD_DIM
    for p in range(N_PAIRS):
        lanes = slice(128 * p, 128 * (p + 1))
        qs = _split_pair(q_ref[:, lanes], even)
        s = _dot_nt(qs, k_ref[:, lanes])
        o_ref[:, lanes] = _softmax_pv([s], [v_ref[:, lanes]], CTX_LEN, even).astype(BF16)


def _context_attention(qkv_ctx):
    blk = lambda col: pl.BlockSpec((CTX_LEN, ATTN_WIDTH), lambda b: (b, col))
    return pl.pallas_call(
        _ctx_attn_kernel,
        grid=(BATCH,),
        in_specs=[blk(0), blk(1), blk(2)],
        out_specs=blk(0),
        out_shape=jax.ShapeDtypeStruct((BATCH * CTX_LEN, ATTN_WIDTH), BF16),
        compiler_params=_params(1),
        name="ctx_attention",
    )(qkv_ctx, qkv_ctx, qkv_ctx)


V7X_MXU_WIDTH = 256
FFN_CHUNKS = ((0, 6 * V7X_MXU_WIDTH), (6 * V7X_MXU_WIDTH, D_FF))


def _outffn_kernel(x_ref, yp_ref, yf_ref, at_ref, mod_ref, wo_ref, g2_ref, w1_ref, w3_ref, w2_ref,
                   *rest, final):
    o_ref = rest[-1]
    y = jnp.concatenate([yp_ref[...], yf_ref[...], at_ref[...]], axis=1)
    x1 = x_ref[...] + mod_ref[0, 2:3, :] * _dot(y, wo_ref[...])
    h = (_rms(x1) * g2_ref[...] * (1.0 + mod_ref[0, 4:5, :]) + mod_ref[0, 3:4, :]).astype(BF16)
    ffn = jnp.zeros_like(x1)
    for lo, hi in FFN_CHUNKS:
        cols = slice(lo, hi)
        a = _dot(h, w1_ref[:, cols])
        gated = (a * jax.nn.sigmoid(a) * _dot(h, w3_ref[:, cols])).astype(BF16)
        ffn = ffn + _dot(gated, w2_ref[cols, :])
    x2 = x1 + mod_ref[0, 5:6, :] * ffn
    if final:
        x2 = _rms(x2) * rest[0][...]
    o_ref[...] = x2


def _outffn(x, yp, yf, at, mod, w_out, g2, w1, w3, w2, final_g, layer, mod_row, tm):
    n = x.shape[0]
    row = lambda w: pl.BlockSpec((tm, w), lambda i: (i, 0))
    in_specs = [row(D_MODEL), row(256), row(256), row(512),
                pl.BlockSpec((1, 6, D_MODEL), lambda i: (mod_row(i, tm), 0, 0)),
                _layer_weight(layer, D_MODEL, D_MODEL),
                _resident((1, D_MODEL), lambda i: (0, 0)),
                _layer_weight(layer, D_MODEL, D_FF),
                _layer_weight(layer, D_MODEL, D_FF),
                _layer_weight(layer, D_FF, D_MODEL)]
    args = [x, yp, yf, at, mod, w_out, g2, w1, w3, w2]
    if final_g is not None:
        in_specs.append(_resident((1, D_MODEL), lambda i: (0, 0)))
        args.append(final_g)
    return pl.pallas_call(
        functools.partial(_outffn_kernel, final=final_g is not None),
        grid=(n // tm,),
        in_specs=in_specs,
        out_specs=row(D_MODEL),
        out_shape=jax.ShapeDtypeStruct((n, D_MODEL), F32),
        compiler_params=_params(1),
        name="outproj_ffn",
    )(*args)


def _latent_mod_row(i, tm):
    return (i * tm) // SEQ


def _ctx_mod_row(i, tm):
    return BATCH


def kernel(x, c, ctx, c_ctx, w_ada, b_ada, norm1_g, w_in, pool_w, pool_scale, fourier_w, nat_bias, w_out,
           norm2_g, w_ffn1, w_ffn3, w_ffn2, final_g):
    cond = jnp.concatenate([c, c_ctx[None, :], jnp.zeros((8 - BATCH - 1, D_MODEL), F32)], axis=0)
    mods = _ada_mod(cond, w_ada, b_ada)
    x = x.reshape(BATCH * SEQ, D_MODEL)
    ctx = ctx.reshape(BATCH * CTX_LEN, D_MODEL)
    w_in, w_out, w_ffn1, w_ffn3, w_ffn2 = (w.astype(BF16) for w in (w_in, w_out, w_ffn1, w_ffn3, w_ffn2))
    for l in range(DEPTH):
        last = l == DEPTH - 1
        mod = mods[l]
        g1 = norm1_g[l][None, :]
        g2 = norm2_g[l][None, :]
        pool_w_bd = _group_block_diag(pool_w[l])
        fourier_w_bd = _group_block_diag(fourier_w[l])
        scale = pool_scale[l][None, :]
        ffn_w = (w_out, g2, w_ffn1, w_ffn3, w_ffn2)

        upc, ufc, qkvc = _inproj(ctx, mod, g1, w_in, l, _ctx_mod_row, 256)
        up, uf, qkv = _inproj(x, mod, g1, w_in, l, _latent_mod_row, 1024)
        yp = _pool(up, pool_w_bd, scale, SEQ, 1024)
        yf = _fourier_latent(uf, fourier_w_bd)
        at = _neighbourhood_attention(qkv, qkvc, _attention_bias(nat_bias[l]))
        x = _outffn(x, yp, yf, at, mod, *ffn_w, final_g[None, :] if last else None, l, _latent_mod_row, 1024)
        if not last:
            ypc = _pool(upc, pool_w_bd, scale, CTX_LEN, CTX_LEN)
            yfc = _fourier_ctx(ufc, fourier_w_bd)
            atc = _context_attention(qkvc)
            ctx = _outffn(ctx, ypc, yfc, atc, mod, *ffn_w, None, l, _ctx_mod_row, 256)
    return x.reshape(BATCH, SEQ, D_MODEL)
```

```python
import functools

import numpy as np
import jax
import jax.numpy as jnp
from jax import lax
from jax.experimental import pallas as pl
from jax.experimental.pallas import tpu as pltpu

D_MODEL = 1024
BATCH = 4
SEQ = 8192
DEPTH = 4
GRID_W = 64
GRID_ROWS = SEQ // GRID_W
CTX_LEN = 256
POOL_WINDOWS = (2, 4, 8, 16)
POOL_WIDTH = 256
FOURIER_WIDTH = 256
GROUP_WIDTH = 64
HEAD_DIM = 64
N_HEADS = 8
N_PAIRS = N_HEADS // 2
ATTN_WIDTH = 512
WIN_R = 8
WIN_C = 16
D_FF = 2816
EPS = 1e-6
NEG_INF = -1e30
LOG2_E = 1.4426950408889634
Q_SCALE = HEAD_DIM ** -0.5 * LOG2_E

FFT_N1 = 128
FFT_N2 = 64
POOL_HALO = 16
BIAS_SHAPE = (N_PAIRS, 2 * WIN_R - 2, 2 * GRID_W, 2 * GRID_W)

V7X_VMEM_LIMIT = 56 * 1024 * 1024
V7X_LANES = 128
PAIR_WIDTH = 2 * HEAD_DIM

F32 = jnp.float32
BF16 = jnp.bfloat16


def _dot(a, b):
    return jnp.dot(a, b, preferred_element_type=F32)


def _dot_nt(a, b):
    return lax.dot_general(a, b, (((1,), (1,)), ((), ())), preferred_element_type=F32)


def _params(n_grid_dims):
    return pltpu.CompilerParams(dimension_semantics=("arbitrary",) * n_grid_dims,
                                vmem_limit_bytes=V7X_VMEM_LIMIT)


def _resident(block_shape, index_map):
    return pl.BlockSpec(block_shape, index_map, pipeline_mode=pl.Buffered(1))


def _block_diag(m, groups):
    w = m.shape[0]
    out = np.zeros((groups * w, groups * w), m.dtype)
    for g in range(groups):
        out[g * w:(g + 1) * w, g * w:(g + 1) * w] = m
    return out


@functools.lru_cache(maxsize=None)
def _dft_constants():
    two_pi = 2.0 * np.pi
    a2 = two_pi * (np.outer(np.arange(FFT_N2), np.arange(FFT_N2)) % FFT_N2) / FFT_N2
    stage1 = np.stack([np.cos(a2), -np.sin(a2)], axis=1).reshape(2 * FFT_N2, FFT_N2)
    k = FFT_N2 * np.arange(FFT_N1)[None, :, None] + np.arange(FFT_N2)[:, None, None]
    n1 = np.arange(FFT_N1)[None, None, :]
    ang = two_pi * ((k * n1) % SEQ) / SEQ
    mr, mi = np.cos(ang), -np.sin(ang)
    stage2 = np.concatenate([np.concatenate([mr, -mi], axis=2),
                             np.concatenate([mi, mr], axis=2)], axis=1)
    ac = two_pi * (np.outer(np.arange(CTX_LEN), np.arange(CTX_LEN)) % CTX_LEN) / CTX_LEN
    ctx_dft = np.concatenate([np.cos(ac), -np.sin(ac)], axis=0)
    ag = two_pi * (np.outer(np.arange(GROUP_WIDTH), np.arange(GROUP_WIDTH)) % GROUP_WIDTH) / GROUP_WIDTH
    chan = np.concatenate([_block_diag(np.cos(ag), 4), _block_diag(np.sin(ag), 4)], axis=0)
    as_bf16 = lambda a: jnp.asarray(a, dtype=F32).astype(BF16)
    return as_bf16(stage1), as_bf16(stage2), as_bf16(ctx_dft), as_bf16(chan)


def _group_block_diag(w):
    out = jnp.zeros((4, GROUP_WIDTH, 4, GROUP_WIDTH), w.dtype)
    for g in range(4):
        out = out.at[g, :, g, :].set(w[g])
    return out.reshape(4 * GROUP_WIDTH, 4 * GROUP_WIDTH).astype(BF16)


def _attention_bias(table):
    col = np.arange(GRID_W)
    cs = np.clip(col - WIN_C // 2, 0, GRID_W - WIN_C)
    col_mask = (col[None, :] >= cs[:, None]) & (col[None, :] < cs[:, None] + WIN_C)
    dc_idx = np.clip(col[None, :] - col[:, None] + WIN_C - 1, 0, 2 * WIN_C - 2)
    onehot = (dc_idx[None] == np.arange(2 * WIN_C - 1)[:, None, None]).astype(np.float32)
    bt = jnp.einsum('hrd,dcj->hrcj', table.astype(F32), jnp.asarray(onehot),
                    precision=lax.Precision.HIGHEST)
    bt = jnp.where(col_mask[None, None], bt * LOG2_E, NEG_INF)
    b = jnp.concatenate([bt[:, :-1], bt[:, 1:]], axis=-1)
    b = b.reshape(N_PAIRS, 2, 2 * WIN_R - 2, GRID_W, 2 * GRID_W).transpose(0, 2, 1, 3, 4)
    return b.reshape(BIAS_SHAPE)


def _ada_kernel(c_ref, w_ref, b_ref, o_ref):
    c = c_ref[...]
    a = (c * jax.nn.sigmoid(c)).astype(BF16)
    o_ref[0] = _dot(a, w_ref[0].astype(BF16)) + b_ref[0]


def _ada_mod(cond, w_ada, b_ada):
    tn = 1536
    out = pl.pallas_call(
        _ada_kernel,
        grid=(DEPTH, 6 * D_MODEL // tn),
        in_specs=[_resident((8, D_MODEL), lambda l, j: (0, 0)),
                  pl.BlockSpec((1, D_MODEL, tn), lambda l, j: (l, 0, j)),
                  pl.BlockSpec((1, 1, tn), lambda l, j: (l, 0, j))],
        out_specs=pl.BlockSpec((1, 8, tn), lambda l, j: (l, 0, j)),
        out_shape=jax.ShapeDtypeStruct((DEPTH, 8, 6 * D_MODEL), F32),
        compiler_params=_params(2),
        name="ada_mod",
    )(cond, w_ada, b_ada.reshape(DEPTH, 1, 6 * D_MODEL))
    return out.reshape(DEPTH, 8, 6, D_MODEL)


def _rms(x):
    return x * lax.rsqrt(jnp.mean(x * x, axis=-1, keepdims=True) + EPS)


P_END = POOL_WIDTH
F_END = P_END + FOURIER_WIDTH
Q_END = F_END + ATTN_WIDTH
K_END = Q_END + ATTN_WIDTH
IN_WIDTH = K_END + ATTN_WIDTH
QKV_WIDTH = 3 * ATTN_WIDTH


def _inproj_kernel(x_ref, mod_ref, g_ref, w_ref, up_ref, uf_ref, qkv_ref):
    y = _rms(x_ref[...]) * g_ref[...]
    h = (y * (1.0 + mod_ref[0, 1:2, :]) + mod_ref[0, 0:1, :]).astype(BF16)
    up_ref[...] = _dot(h, w_ref[:, 0:P_END]).astype(BF16)
    uf_ref[...] = _dot(h, w_ref[:, P_END:F_END]).astype(BF16)
    qkv_ref[:, 0:ATTN_WIDTH] = (_dot(h, w_ref[:, F_END:Q_END]) * Q_SCALE).astype(BF16)
    qkv_ref[:, ATTN_WIDTH:2 * ATTN_WIDTH] = _dot(h, w_ref[:, Q_END:K_END]).astype(BF16)
    qkv_ref[:, 2 * ATTN_WIDTH:QKV_WIDTH] = _dot(h, w_ref[:, K_END:IN_WIDTH]).astype(BF16)


def _layer_weight(layer, rows, cols):
    return _resident((None, rows, cols), lambda i: (layer, 0, 0))


def _inproj(x, mod, g, w_in, layer, mod_row, tm):
    n = x.shape[0]
    return pl.pallas_call(
        _inproj_kernel,
        grid=(n // tm,),
        in_specs=[pl.BlockSpec((tm, D_MODEL), lambda i: (i, 0)),
                  pl.BlockSpec((1, 6, D_MODEL), lambda i: (mod_row(i, tm), 0, 0)),
                  _resident((1, D_MODEL), lambda i: (0, 0)),
                  _layer_weight(layer, D_MODEL, IN_WIDTH)],
        out_specs=[pl.BlockSpec((tm, POOL_WIDTH), lambda i: (i, 0)),
                   pl.BlockSpec((tm, FOURIER_WIDTH), lambda i: (i, 0)),
                   pl.BlockSpec((tm, QKV_WIDTH), lambda i: (i, 0))],
        out_shape=[jax.ShapeDtypeStruct((n, POOL_WIDTH), BF16),
                   jax.ShapeDtypeStruct((n, FOURIER_WIDTH), BF16),
                   jax.ShapeDtypeStruct((n, QKV_WIDTH), BF16)],
        compiler_params=_params(1),
        name="inproj",
    )(x, mod, g, w_in)


POOL_SUB = 256
POOL_BAND_K = POOL_SUB + 2 * POOL_HALO


@functools.lru_cache(maxsize=None)
def _pool_bands():
    t = np.arange(POOL_SUB)[:, None]
    d = np.arange(POOL_BAND_K)[None, :] - POOL_HALO - t
    member = [((d >= -(w // 2)) & (d < w - w // 2)).astype(np.float64) for w in POOL_WINDOWS]
    centred = [m / w - (d == 0) for m, w in zip(member, POOL_WINDOWS)]
    interior = np.stack([np.concatenate(centred[0:2], axis=1), np.concatenate(centred[2:4], axis=1)], axis=0)
    as_bf16 = lambda a: jnp.asarray(a, dtype=F32).astype(BF16)
    return as_bf16(np.concatenate(member, axis=0)), as_bf16(interior)


def _pool_kernel(band_ref, inner_ref, prev_ref, cur_ref, next_ref, pw_ref, ps_ref, o_ref, *, seq_tiles, nsub):
    st = pl.program_id(0) % seq_tiles
    ub_all = jnp.concatenate([prev_ref[...], cur_ref[...], next_ref[...]], axis=0)
    left = lax.broadcasted_iota(jnp.int32, (POOL_BAND_K, V7X_LANES), 1) < GROUP_WIDTH

    for s in range(nsub):
        ub = ub_all[s * POOL_SUB:s * POOL_SUB + POOL_BAND_K]
        cur = ub[POOL_HALO:POOL_HALO + POOL_SUB]

        def interior(ub=ub):
            halves = []
            for h in range(2):
                t = ub[:, V7X_LANES * h:V7X_LANES * (h + 1)]
                zero = jnp.zeros_like(t)
                stacked = jnp.concatenate([jnp.where(left, t, zero), jnp.where(left, zero, t)], axis=0)
                halves.append(_dot(inner_ref[h], stacked))
            return jnp.concatenate(halves, axis=1).astype(BF16)

        def clipped(ub=ub, cur=cur, s=s):
            lo_b = jnp.where(st == 0, 0, -POOL_HALO) if s == 0 else -POOL_HALO
            hi_b = (jnp.where(st == seq_tiles - 1, POOL_SUB, POOL_SUB + POOL_HALO) if s == nsub - 1
                    else POOL_SUB + POOL_HALO)
            p = lax.broadcasted_iota(jnp.int32, ub.shape, 0) - POOL_HALO
            ubv = jnp.where((p >= lo_b) & (p < hi_b), ub, jnp.zeros_like(ub))
            sums = _dot(band_ref[...], ubv)
            tcol = lax.broadcasted_iota(jnp.int32, (POOL_SUB, 1), 0)
            group = lax.broadcasted_iota(jnp.int32, (POOL_SUB, POOL_WIDTH), 1) // GROUP_WIDTH
            mean = jnp.zeros((POOL_SUB, POOL_WIDTH), F32)
            for g, w in enumerate(POOL_WINDOWS):
                cnt = jnp.minimum(tcol + (w - w // 2), hi_b) - jnp.maximum(tcol - w // 2, lo_b)
                mean = jnp.where(group == g, sums[g * POOL_SUB:(g + 1) * POOL_SUB] / cnt.astype(F32), mean)
            return (mean - cur.astype(F32)).astype(BF16)

        at_start = (st == 0) if s == 0 else None
        at_end = (st == seq_tiles - 1) if s == nsub - 1 else None
        if at_start is None and at_end is None:
            diff = interior()
        else:
            edge = at_start if at_end is None else (at_end if at_start is None else at_start | at_end)
            diff = lax.cond(edge, clipped, interior)
        o_ref[s * POOL_SUB:(s + 1) * POOL_SUB] = (_dot(diff, pw_ref[...]) * ps_ref[...]).astype(BF16)


def _pool(up, pool_w_bd, pool_scale, seq_len, tm):
    n = up.shape[0]
    hb = tm // POOL_HALO
    n_halo_blocks = n // POOL_HALO
    bands, inner = _pool_bands()
    return pl.pallas_call(
        functools.partial(_pool_kernel, seq_tiles=seq_len // tm, nsub=tm // POOL_SUB),
        grid=(n // tm,),
        in_specs=[_resident(bands.shape, lambda i: (0, 0)),
                  _resident(inner.shape, lambda i: (0, 0, 0)),
                  pl.BlockSpec((POOL_HALO, POOL_WIDTH), lambda i: (jnp.maximum(i * hb - 1, 0), 0)),
                  pl.BlockSpec((tm, POOL_WIDTH), lambda i: (i, 0)),
                  pl.BlockSpec((POOL_HALO, POOL_WIDTH),
                               lambda i: (jnp.minimum((i + 1) * hb, n_halo_blocks - 1), 0)),
                  _resident((POOL_WIDTH, POOL_WIDTH), lambda i: (0, 0)),
                  _resident((1, POOL_WIDTH), lambda i: (0, 0))],
        out_specs=pl.BlockSpec((tm, POOL_WIDTH), lambda i: (i, 0)),
        out_shape=jax.ShapeDtypeStruct((n, POOL_WIDTH), BF16),
        compiler_params=_params(1),
        name="pool",
    )(bands, inner, up, up, up, pool_w_bd, pool_scale)


FFT_CHUNK = 32


def _fft1_kernel(m_ref, x_ref, y_ref):
    xt = jnp.transpose(x_ref[...], (1, 0, 2))
    y = jnp.stack([_dot(m_ref[...], xt[n1]).astype(BF16) for n1 in range(FFT_CHUNK)], axis=0)
    y_ref[...] = jnp.transpose(y, (1, 0, 2)).reshape(y_ref.shape)


def _fourier_epilogue(zz, cs_ref, w_ref, norm):
    f = (_dot(zz, cs_ref[...]) * norm).astype(BF16)
    return _dot(f, w_ref[...])


def _fft2_kernel(t_ref, y_ref, cs_ref, w_ref, o_ref, zz, *, norm):
    for kk in range(FFT_CHUNK):
        z = _dot(t_ref[kk], y_ref[kk].reshape(2 * FFT_N1, FOURIER_WIDTH))
        zz[kk * FFT_N1:(kk + 1) * FFT_N1, :] = jnp.concatenate([z[:FFT_N1], z[FFT_N1:]], axis=1).astype(BF16)
    o = _fourier_epilogue(zz[...], cs_ref, w_ref, norm).astype(BF16)
    o_ref[...] = jnp.transpose(o.reshape(FFT_CHUNK, FFT_N1, FOURIER_WIDTH), (1, 0, 2))


def _fourier_latent(uf, fourier_w_bd):
    stage1, stage2, _, chan = _dft_constants()
    chunks1 = FFT_N1 // FFT_CHUNK
    y = pl.pallas_call(
        _fft1_kernel,
        grid=(BATCH, chunks1),
        in_specs=[_resident((2 * FFT_N2, FFT_N2), lambda b, j: (0, 0)),
                  pl.BlockSpec((FFT_N2, FFT_CHUNK, FOURIER_WIDTH), lambda b, j: (b, j, 0))],
        out_specs=pl.BlockSpec((FFT_N2, 2, FFT_CHUNK, FOURIER_WIDTH), lambda b, j: (b, 0, j, 0)),
        out_shape=jax.ShapeDtypeStruct((BATCH * FFT_N2, 2, FFT_N1, FOURIER_WIDTH), BF16),
        compiler_params=_params(2),
        name="fft_stage1",
    )(stage1, uf.reshape(BATCH * FFT_N2, FFT_N1, FOURIER_WIDTH))
    chunks2 = FFT_N2 // FFT_CHUNK
    out = pl.pallas_call(
        functools.partial(_fft2_kernel, norm=float((SEQ * GROUP_WIDTH) ** -0.5)),
        grid=(chunks2, BATCH),
        in_specs=[pl.BlockSpec((FFT_CHUNK, 2 * FFT_N1, 2 * FFT_N1), lambda j, b: (j, 0, 0)),
                  pl.BlockSpec((FFT_CHUNK, 2, FFT_N1, FOURIER_WIDTH), lambda j, b: (b * chunks2 + j, 0, 0, 0)),
                  _resident((2 * FOURIER_WIDTH, FOURIER_WIDTH), lambda j, b: (0, 0)),
                  _resident((FOURIER_WIDTH, FOURIER_WIDTH), lambda j, b: (0, 0))],
        out_specs=pl.BlockSpec((FFT_N1, FFT_CHUNK, FOURIER_WIDTH), lambda j, b: (b, j, 0)),
        out_shape=jax.ShapeDtypeStruct((BATCH * FFT_N1, FFT_N2, FOURIER_WIDTH), BF16),
        scratch_shapes=[pltpu.VMEM((FFT_CHUNK * FFT_N1, 2 * FOURIER_WIDTH), BF16)],
        compiler_params=_params(2),
        name="fft_stage2",
    )(stage2, y, chan, fourier_w_bd)
    return out.reshape(BATCH * SEQ, FOURIER_WIDTH)


def _fourier_ctx_kernel(m_ref, u_ref, cs_ref, w_ref, o_ref, *, norm):
    z = _dot(m_ref[...], u_ref[...])
    zz = jnp.concatenate([z[:CTX_LEN], z[CTX_LEN:]], axis=1).astype(BF16)
    o_ref[...] = _fourier_epilogue(zz, cs_ref, w_ref, norm).astype(BF16)


def _fourier_ctx(ufc, fourier_w_bd):
    _, _, ctx_dft, chan = _dft_constants()
    return pl.pallas_call(
        functools.partial(_fourier_ctx_kernel, norm=float((CTX_LEN * GROUP_WIDTH) ** -0.5)),
        grid=(BATCH,),
        in_specs=[_resident((2 * CTX_LEN, CTX_LEN), lambda b: (0, 0)),
                  pl.BlockSpec((CTX_LEN, FOURIER_WIDTH), lambda b: (b, 0)),
                  _resident((2 * FOURIER_WIDTH, FOURIER_WIDTH), lambda b: (0, 0)),
                  _resident((FOURIER_WIDTH, FOURIER_WIDTH), lambda b: (0, 0))],
        out_specs=pl.BlockSpec((CTX_LEN, FOURIER_WIDTH), lambda b: (b, 0)),
        out_shape=jax.ShapeDtypeStruct((BATCH * CTX_LEN, FOURIER_WIDTH), BF16),
        compiler_params=_params(1),
        name="fourier_ctx",
    )(ctx_dft, ufc, chan, fourier_w_bd)


def _split_pair(x, even):
    zero = jnp.zeros_like(x)
    return jnp.concatenate([jnp.where(even, x, zero), jnp.where(even, zero, x)], axis=0)


def _softmax_pv(scores, values, n_q, even):
    m = functools.reduce(jnp.maximum, [jnp.max(s, axis=1, keepdims=True) for s in scores])
    e = [jnp.exp2(s - m) for s in scores]
    l = functools.reduce(jnp.add, [jnp.sum(x, axis=1, keepdims=True) for x in e])
    o = functools.reduce(jnp.add, [_dot(x.astype(BF16), v) for x, v in zip(e, values)])
    o = o / l
    return jnp.where(even, o[:n_q], o[n_q:])


ATTN_TILE_ROWS = 32
ATTN_TILE = ATTN_TILE_ROWS * GRID_W
ATTN_HALO = WIN_R * GRID_W
ATTN_ROWS_PER_TRIP = 4
SOFTMAX_ROWS = 32


def _kv_window_row(t):
    return jnp.clip(t * ATTN_TILE_ROWS - WIN_R, 0, GRID_ROWS - ATTN_TILE_ROWS - 2 * WIN_R)


def _nat_kernel(q_ref, kbuf, vbuf, kx_ref, vx_ref, bias_ref, o_ref, sbuf_a, sbuf_b, pbuf):
    t = pl.program_id(1)
    even = lax.broadcasted_iota(jnp.int32, (GRID_W, PAIR_WIDTH), 1) < HEAD_DIM
    n_win = WIN_R * GRID_W

    def window_offset(j):
        r = t * ATTN_TILE_ROWS + j
        rs = jnp.clip(r - WIN_R // 2, 0, GRID_ROWS - WIN_R)
        return pl.multiple_of((rs - _kv_window_row(t)) * GRID_W, GRID_W)

    def query_offset(j):
        return j * GRID_W if isinstance(j, int) else pl.multiple_of(j * GRID_W, GRID_W)

    def scores(j, sbuf):
        r = t * ATTN_TILE_ROWS + j
        off = window_offset(j)
        dr0 = jnp.clip(r - WIN_R // 2, 0, GRID_ROWS - WIN_R) - r + WIN_R - 1
        qoff = query_offset(j)
        for p in range(N_PAIRS):
            lanes = slice(PAIR_WIDTH * p, PAIR_WIDTH * (p + 1))
            qs = _split_pair(q_ref[pl.ds(qoff, GRID_W), lanes], even)
            bias = jnp.concatenate([bias_ref[p, dr0 + 2 * m] for m in range(WIN_R // 2)], axis=1)
            sbuf[p, :, 0:n_win] = _dot_nt(qs, kbuf[pl.ds(off, n_win), lanes]) + bias
            sbuf[p, :, n_win:] = _dot_nt(qs, kx_ref[:, lanes])

    def outputs(j, sbuf):
        off = window_offset(j)
        qoff = query_offset(j)
        for p in range(N_PAIRS):
            lanes = slice(PAIR_WIDTH * p, PAIR_WIDTH * (p + 1))
            sums = []
            for g in range(2 * GRID_W // SOFTMAX_ROWS):
                rows = slice(g * SOFTMAX_ROWS, (g + 1) * SOFTMAX_ROWS)
                s = sbuf[p, rows, :]
                e = jnp.exp2(s - jnp.max(s, axis=1, keepdims=True))
                sums.append(jnp.sum(e, axis=1, keepdims=True))
                pbuf[p, rows, :] = e.astype(BF16)
            l = jnp.concatenate(sums, axis=0)
            o = (_dot(pbuf[p, :, :n_win], vbuf[pl.ds(off, n_win), lanes])
                 + _dot(pbuf[p, :, n_win:], vx_ref[:, lanes]))
            o = o / l
            o_ref[pl.ds(qoff, GRID_W), lanes] = jnp.where(even, o[:GRID_W], o[GRID_W:]).astype(BF16)

    scores(0, sbuf_a)

    def body(i, carry):
        base = ATTN_ROWS_PER_TRIP * i
        for k in range(ATTN_ROWS_PER_TRIP):
            cur, nxt = (sbuf_a, sbuf_b) if k % 2 == 0 else (sbuf_b, sbuf_a)
            scores(jnp.minimum(base + k + 1, ATTN_TILE_ROWS - 1), nxt)
            outputs(base + k, cur)
        return carry

    lax.fori_loop(0, ATTN_TILE_ROWS // ATTN_ROWS_PER_TRIP, body, 0)


def _neighbourhood_attention(qkv, qkv_ctx, bias):
    tiles = SEQ // ATTN_TILE
    cur = lambda col: (lambda b, t: (b * tiles + t, col))
    blk = lambda f: pl.BlockSpec((ATTN_TILE, ATTN_WIDTH), f)
    kv = lambda col: pl.BlockSpec(
        (pl.Element(ATTN_TILE + 2 * ATTN_HALO), pl.Element(ATTN_WIDTH)),
        lambda b, t: (pl.multiple_of(b * SEQ + _kv_window_row(t) * GRID_W, GRID_W), col * ATTN_WIDTH))
    return pl.pallas_call(
        _nat_kernel,
        grid=(BATCH, tiles),
        in_specs=[blk(cur(0)), kv(1), kv(2),
                  pl.BlockSpec((CTX_LEN, ATTN_WIDTH), lambda b, t: (b, 1)),
                  pl.BlockSpec((CTX_LEN, ATTN_WIDTH), lambda b, t: (b, 2)),
                  _resident(bias.shape, lambda b, t: (0, 0, 0, 0))],
        out_specs=blk(cur(0)),
        out_shape=jax.ShapeDtypeStruct((BATCH * SEQ, ATTN_WIDTH), BF16),
        scratch_shapes=[pltpu.VMEM((N_PAIRS, 2 * GRID_W, WIN_R * GRID_W + CTX_LEN), F32),
                        pltpu.VMEM((N_PAIRS, 2 * GRID_W, WIN_R * GRID_W + CTX_LEN), F32),
                        pltpu.VMEM((N_PAIRS, 2 * GRID_W, WIN_R * GRID_W + CTX_LEN), BF16)],
        compiler_params=_params(2),
        name="nat_attention",
    )(qkv, qkv, qkv, qkv_ctx, qkv_ctx, bias)


def _ctx_attn_kernel(q_ref, k_ref, v_ref, o_ref):
    even = lax.broadcasted_iota(jnp.int32, (CTX_LEN, PAIR_WIDTH), 1) < HEAD_DIM
    for p in range(N_PAIRS):
        lanes = slice(PAIR_WIDTH * p, PAIR_WIDTH * (p + 1))
        qs = _split_pair(q_ref[:, lanes], even)
        s = _dot_nt(qs, k_ref[:, lanes])
        o_ref[:, lanes] = _softmax_pv([s], [v_ref[:, lanes]], CTX_LEN, even).astype(BF16)


def _context_attention(qkv_ctx):
    blk = lambda col: pl.BlockSpec((CTX_LEN, ATTN_WIDTH), lambda b: (b, col))
    return pl.pallas_call(
        _ctx_attn_kernel,
        grid=(BATCH,),
        in_specs=[blk(0), blk(1), blk(2)],
        out_specs=blk(0),
        out_shape=jax.ShapeDtypeStruct((BATCH * CTX_LEN, ATTN_WIDTH), BF16),
        compiler_params=_params(1),
        name="ctx_attention",
    )(qkv_ctx, qkv_ctx, qkv_ctx)


V7X_MXU_WIDTH = 256
FFN_CHUNKS = ((0, 6 * V7X_MXU_WIDTH), (6 * V7X_MXU_WIDTH, D_FF))


def _outffn_kernel(x_ref, yp_ref, yf_ref, at_ref, mod_ref, wo_ref, g2_ref, w1_ref, w3_ref, w2_ref,
                   *rest, final):
    o_ref = rest[-1]
    y = jnp.concatenate([yp_ref[...], yf_ref[...], at_ref[...]], axis=1)
    x1 = x_ref[...] + mod_ref[0, 2:3, :] * _dot(y, wo_ref[...])
    h = (_rms(x1) * g2_ref[...] * (1.0 + mod_ref[0, 4:5, :]) + mod_ref[0, 3:4, :]).astype(BF16)
    ffn = jnp.zeros_like(x1)
    for lo, hi in FFN_CHUNKS:
        cols = slice(lo, hi)
        a = _dot(h, w1_ref[:, cols])
        gated = (a * jax.nn.sigmoid(a) * _dot(h, w3_ref[:, cols])).astype(BF16)
        ffn = ffn + _dot(gated, w2_ref[cols, :])
    x2 = x1 + mod_ref[0, 5:6, :] * ffn
    if final:
        x2 = _rms(x2) * rest[0][...]
    o_ref[...] = x2


def _outffn(x, yp, yf, at, mod, w_out, g2, w1, w3, w2, final_g, layer, mod_row, tm):
    n = x.shape[0]
    row = lambda w: pl.BlockSpec((tm, w), lambda i: (i, 0))
    in_specs = [row(D_MODEL), row(POOL_WIDTH), row(FOURIER_WIDTH), row(ATTN_WIDTH),
                pl.BlockSpec((1, 6, D_MODEL), lambda i: (mod_row(i, tm), 0, 0)),
                _layer_weight(layer, D_MODEL, D_MODEL),
                _resident((1, D_MODEL), lambda i: (0, 0)),
                _layer_weight(layer, D_MODEL, D_FF),
                _layer_weight(layer, D_MODEL, D_FF),
                _layer_weight(layer, D_FF, D_MODEL)]
    args = [x, yp, yf, at, mod, w_out, g2, w1, w3, w2]
    if final_g is not None:
        in_specs.append(_resident((1, D_MODEL), lambda i: (0, 0)))
        args.append(final_g)
    return pl.pallas_call(
        functools.partial(_outffn_kernel, final=final_g is not None),
        grid=(n // tm,),
        in_specs=in_specs,
        out_specs=row(D_MODEL),
        out_shape=jax.ShapeDtypeStruct((n, D_MODEL), F32),
        compiler_params=_params(1),
        name="outproj_ffn",
    )(*args)


def _latent_mod_row(i, tm):
    return (i * tm) // SEQ


def _ctx_mod_row(i, tm):
    return BATCH


def kernel(x, c, ctx, c_ctx, w_ada, b_ada, norm1_g, w_in, pool_w, pool_scale, fourier_w, nat_bias, w_out,
           norm2_g, w_ffn1, w_ffn3, w_ffn2, final_g):
    cond = jnp.concatenate([c, c_ctx[None, :], jnp.zeros((8 - BATCH - 1, D_MODEL), F32)], axis=0)
    mods = _ada_mod(cond, w_ada, b_ada)
    x = x.reshape(BATCH * SEQ, D_MODEL)
    ctx = ctx.reshape(BATCH * CTX_LEN, D_MODEL)
    w_in, w_out, w_ffn1, w_ffn3, w_ffn2 = (w.astype(BF16) for w in (w_in, w_out, w_ffn1, w_ffn3, w_ffn2))
    for l in range(DEPTH):
        last = l == DEPTH - 1
        mod = mods[l]
        g1 = norm1_g[l][None, :]
        g2 = norm2_g[l][None, :]
        pool_w_bd = _group_block_diag(pool_w[l])
        fourier_w_bd = _group_block_diag(fourier_w[l])
        scale = pool_scale[l][None, :]
        ffn_w = (w_out, g2, w_ffn1, w_ffn3, w_ffn2)

        upc, ufc, qkvc = _inproj(ctx, mod, g1, w_in, l, _ctx_mod_row, 512)
        up, uf, qkv = _inproj(x, mod, g1, w_in, l, _latent_mod_row, 1024)
        yp = _pool(up, pool_w_bd, scale, SEQ, SEQ)
        yf = _fourier_latent(uf, fourier_w_bd)
        at = _neighbourhood_attention(qkv, qkvc, _attention_bias(nat_bias[l]))
        x = _outffn(x, yp, yf, at, mod, *ffn_w, final_g[None, :] if last else None, l, _latent_mod_row, 1024)
        if not last:
            ypc = _pool(upc, pool_w_bd, scale, CTX_LEN, CTX_LEN)
            yfc = _fourier_ctx(ufc, fourier_w_bd)
            atc = _context_attention(qkvc)
            ctx = _outffn(ctx, ypc, yfc, atc, mod, *ffn_w, None, l, _ctx_mod_row, 512)
    return x.reshape(BATCH, SEQ, D_MODEL)
```

```python
import functools

import numpy as np
import jax
import jax.numpy as jnp
from jax import lax
from jax.experimental import pallas as pl
from jax.experimental.pallas import tpu as pltpu

D_MODEL = 1024
BATCH = 4
SEQ = 8192
DEPTH = 4
GRID_W = 64
GRID_ROWS = SEQ // GRID_W
CTX_LEN = 256
POOL_WINDOWS = (2, 4, 8, 16)
POOL_WIDTH = 256
FOURIER_WIDTH = 256
GROUP_WIDTH = 64
HEAD_DIM = 64
N_HEADS = 8
N_PAIRS = N_HEADS // 2
ATTN_WIDTH = 512
WIN_R = 8
WIN_C = 16
D_FF = 2816
EPS = 1e-6
NEG_INF = -1e30
LOG2_E = 1.4426950408889634
Q_SCALE = HEAD_DIM ** -0.5 * LOG2_E

FFT_N1 = 128
FFT_N2 = 64
POOL_HALO = 16
BIAS_SHAPE = (N_PAIRS, 2 * WIN_R - 2, 2 * GRID_W, 2 * GRID_W)

V7X_VMEM_LIMIT = 56 * 1024 * 1024
V7X_LANES = 128
PAIR_WIDTH = 2 * HEAD_DIM

F32 = jnp.float32
BF16 = jnp.bfloat16


def _dot(a, b):
    return jnp.dot(a, b, preferred_element_type=F32)


def _dot_nt(a, b):
    return lax.dot_general(a, b, (((1,), (1,)), ((), ())), preferred_element_type=F32)


def _params(n_grid_dims):
    return pltpu.CompilerParams(dimension_semantics=("arbitrary",) * n_grid_dims,
                                vmem_limit_bytes=V7X_VMEM_LIMIT)


def _resident(block_shape, index_map):
    return pl.BlockSpec(block_shape, index_map, pipeline_mode=pl.Buffered(1))


def _block_diag(m, groups):
    w = m.shape[0]
    out = np.zeros((groups * w, groups * w), m.dtype)
    for g in range(groups):
        out[g * w:(g + 1) * w, g * w:(g + 1) * w] = m
    return out


@functools.lru_cache(maxsize=None)
def _dft_constants():
    two_pi = 2.0 * np.pi
    a2 = two_pi * (np.outer(np.arange(FFT_N2), np.arange(FFT_N2)) % FFT_N2) / FFT_N2
    stage1 = np.stack([np.cos(a2), -np.sin(a2)], axis=1).reshape(2 * FFT_N2, FFT_N2)
    k = FFT_N2 * np.arange(FFT_N1)[None, :, None] + np.arange(FFT_N2)[:, None, None]
    n1 = np.arange(FFT_N1)[None, None, :]
    ang = two_pi * ((k * n1) % SEQ) / SEQ
    mr, mi = np.cos(ang), -np.sin(ang)
    stage2 = np.concatenate([np.concatenate([mr, -mi], axis=2),
                             np.concatenate([mi, mr], axis=2)], axis=1)
    ac = two_pi * (np.outer(np.arange(CTX_LEN), np.arange(CTX_LEN)) % CTX_LEN) / CTX_LEN
    ctx_dft = np.concatenate([np.cos(ac), -np.sin(ac)], axis=0)
    ag = two_pi * (np.outer(np.arange(GROUP_WIDTH), np.arange(GROUP_WIDTH)) % GROUP_WIDTH) / GROUP_WIDTH
    chan = np.concatenate([_block_diag(np.cos(ag), 4), _block_diag(np.sin(ag), 4)], axis=0)
    as_bf16 = lambda a: jnp.asarray(a, dtype=F32).astype(BF16)
    return as_bf16(stage1), as_bf16(stage2), as_bf16(ctx_dft), as_bf16(chan)


def _group_block_diag(w):
    out = jnp.zeros((4, GROUP_WIDTH, 4, GROUP_WIDTH), w.dtype)
    for g in range(4):
        out = out.at[g, :, g, :].set(w[g])
    return out.reshape(4 * GROUP_WIDTH, 4 * GROUP_WIDTH).astype(BF16)


def _attention_bias(table):
    col = np.arange(GRID_W)
    cs = np.clip(col - WIN_C // 2, 0, GRID_W - WIN_C)
    col_mask = (col[None, :] >= cs[:, None]) & (col[None, :] < cs[:, None] + WIN_C)
    dc_idx = np.clip(col[None, :] - col[:, None] + WIN_C - 1, 0, 2 * WIN_C - 2)
    onehot = (dc_idx[None] == np.arange(2 * WIN_C - 1)[:, None, None]).astype(np.float32)
    bt = jnp.einsum('hrd,dcj->hrcj', table.astype(F32), jnp.asarray(onehot),
                    precision=lax.Precision.HIGHEST)
    bt = jnp.where(col_mask[None, None], bt * LOG2_E, NEG_INF)
    b = jnp.concatenate([bt[:, :-1], bt[:, 1:]], axis=-1)
    b = b.reshape(N_PAIRS, 2, 2 * WIN_R - 2, GRID_W, 2 * GRID_W).transpose(0, 2, 1, 3, 4)
    return b.reshape(BIAS_SHAPE)


def _ada_kernel(c_ref, w_ref, b_ref, o_ref):
    c = c_ref[...]
    a = (c * jax.nn.sigmoid(c)).astype(BF16)
    o_ref[0] = _dot(a, w_ref[0].astype(BF16)) + b_ref[0]


def _ada_mod(cond, w_ada, b_ada):
    tn = 1536
    out = pl.pallas_call(
        _ada_kernel,
        grid=(DEPTH, 6 * D_MODEL // tn),
        in_specs=[_resident((8, D_MODEL), lambda l, j: (0, 0)),
                  pl.BlockSpec((1, D_MODEL, tn), lambda l, j: (l, 0, j)),
                  pl.BlockSpec((1, 1, tn), lambda l, j: (l, 0, j))],
        out_specs=pl.BlockSpec((1, 8, tn), lambda l, j: (l, 0, j)),
        out_shape=jax.ShapeDtypeStruct((DEPTH, 8, 6 * D_MODEL), F32),
        compiler_params=_params(2),
        name="ada_mod",
    )(cond, w_ada, b_ada.reshape(DEPTH, 1, 6 * D_MODEL))
    return out.reshape(DEPTH, 8, 6, D_MODEL)


def _rms(x):
    return x * lax.rsqrt(jnp.mean(x * x, axis=-1, keepdims=True) + EPS)


P_END = POOL_WIDTH
F_END = P_END + FOURIER_WIDTH
Q_END = F_END + ATTN_WIDTH
K_END = Q_END + ATTN_WIDTH
IN_WIDTH = K_END + ATTN_WIDTH
QKV_WIDTH = 3 * ATTN_WIDTH


def _inproj_kernel(x_ref, mod_ref, g_ref, w_ref, *refs):
    n_cast = (len(refs) - 3) // 2
    up_ref, uf_ref, qkv_ref = refs[n_cast:n_cast + 3]
    for src, dst in zip(refs[:n_cast], refs[n_cast + 3:]):
        dst[...] = src[...].astype(BF16)
    y = _rms(x_ref[...]) * g_ref[...]
    h = (y * (1.0 + mod_ref[0, 1:2, :]) + mod_ref[0, 0:1, :]).astype(BF16)
    up_ref[...] = _dot(h, w_ref[:, 0:P_END]).astype(BF16)
    uf_ref[...] = _dot(h, w_ref[:, P_END:F_END]).astype(BF16)
    qkv_ref[:, 0:ATTN_WIDTH] = (_dot(h, w_ref[:, F_END:Q_END]) * Q_SCALE).astype(BF16)
    qkv_ref[:, ATTN_WIDTH:2 * ATTN_WIDTH] = _dot(h, w_ref[:, Q_END:K_END]).astype(BF16)
    qkv_ref[:, 2 * ATTN_WIDTH:QKV_WIDTH] = _dot(h, w_ref[:, K_END:IN_WIDTH]).astype(BF16)


def _layer_weight(layer, rows, cols):
    return _resident((None, rows, cols), lambda i: (layer, 0, 0))


BF16_SUBLANES = 16


def _inproj(x, mod, g, w_in, layer, mod_row, tm, cast=()):
    n = x.shape[0]
    steps = n // tm
    cast_in, cast_out, cast_shape = [], [], []
    for w in cast:
        _, rows, cols = w.shape
        share = next(k for k in (1, 2, 4, 8) if (rows * k) % (steps * BF16_SUBLANES) == 0)
        slab = rows * share // steps
        cast_in.append(pl.BlockSpec((None, slab, cols), lambda i, share=share: (layer, i // share, 0)))
        cast_out.append(pl.BlockSpec((slab, cols), lambda i, share=share: (i // share, 0)))
        cast_shape.append(jax.ShapeDtypeStruct((rows, cols), BF16))
    return pl.pallas_call(
        _inproj_kernel,
        grid=(steps,),
        in_specs=[pl.BlockSpec((tm, D_MODEL), lambda i: (i, 0)),
                  pl.BlockSpec((1, 6, D_MODEL), lambda i: (mod_row(i, tm), 0, 0)),
                  _resident((1, D_MODEL), lambda i: (0, 0)),
                  _layer_weight(layer, D_MODEL, IN_WIDTH)] + cast_in,
        out_specs=[pl.BlockSpec((tm, POOL_WIDTH), lambda i: (i, 0)),
                   pl.BlockSpec((tm, FOURIER_WIDTH), lambda i: (i, 0)),
                   pl.BlockSpec((tm, QKV_WIDTH), lambda i: (i, 0))] + cast_out,
        out_shape=[jax.ShapeDtypeStruct((n, POOL_WIDTH), BF16),
                   jax.ShapeDtypeStruct((n, FOURIER_WIDTH), BF16),
                   jax.ShapeDtypeStruct((n, QKV_WIDTH), BF16)] + cast_shape,
        compiler_params=_params(1),
        name="inproj",
    )(x, mod, g, w_in, *cast)


POOL_SUB = 256
POOL_BAND_K = POOL_SUB + 2 * POOL_HALO


@functools.lru_cache(maxsize=None)
def _pool_bands():
    t = np.arange(POOL_SUB)[:, None]
    d = np.arange(POOL_BAND_K)[None, :] - POOL_HALO - t
    member = [((d >= -(w // 2)) & (d < w - w // 2)).astype(np.float64) for w in POOL_WINDOWS]
    centred = [m / w - (d == 0) for m, w in zip(member, POOL_WINDOWS)]
    interior = np.stack([np.concatenate(centred[0:2], axis=1), np.concatenate(centred[2:4], axis=1)], axis=0)
    as_bf16 = lambda a: jnp.asarray(a, dtype=F32).astype(BF16)
    return as_bf16(np.concatenate(member, axis=0)), as_bf16(interior)


def _pool_kernel(band_ref, inner_ref, prev_ref, cur_ref, next_ref, pw_ref, ps_ref, o_ref, *, seq_tiles, nsub):
    st = pl.program_id(0) % seq_tiles
    ub_all = jnp.concatenate([prev_ref[...], cur_ref[...], next_ref[...]], axis=0)
    left = lax.broadcasted_iota(jnp.int32, (POOL_BAND_K, V7X_LANES), 1) < GROUP_WIDTH

    for s in range(nsub):
        ub = ub_all[s * POOL_SUB:s * POOL_SUB + POOL_BAND_K]
        cur = ub[POOL_HALO:POOL_HALO + POOL_SUB]

        def interior(ub=ub):
            halves = []
            for h in range(2):
                t = ub[:, V7X_LANES * h:V7X_LANES * (h + 1)]
                zero = jnp.zeros_like(t)
                stacked = jnp.concatenate([jnp.where(left, t, zero), jnp.where(left, zero, t)], axis=0)
                halves.append(_dot(inner_ref[h], stacked))
            return jnp.concatenate(halves, axis=1).astype(BF16)

        def clipped(ub=ub, cur=cur, s=s):
            lo_b = jnp.where(st == 0, 0, -POOL_HALO) if s == 0 else -POOL_HALO
            hi_b = (jnp.where(st == seq_tiles - 1, POOL_SUB, POOL_SUB + POOL_HALO) if s == nsub - 1
                    else POOL_SUB + POOL_HALO)
            p = lax.broadcasted_iota(jnp.int32, ub.shape, 0) - POOL_HALO
            ubv = jnp.where((p >= lo_b) & (p < hi_b), ub, jnp.zeros_like(ub))
            sums = _dot(band_ref[...], ubv)
            tcol = lax.broadcasted_iota(jnp.int32, (POOL_SUB, 1), 0)
            group = lax.broadcasted_iota(jnp.int32, (POOL_SUB, POOL_WIDTH), 1) // GROUP_WIDTH
            mean = jnp.zeros((POOL_SUB, POOL_WIDTH), F32)
            for g, w in enumerate(POOL_WINDOWS):
                cnt = jnp.minimum(tcol + (w - w // 2), hi_b) - jnp.maximum(tcol - w // 2, lo_b)
                mean = jnp.where(group == g, sums[g * POOL_SUB:(g + 1) * POOL_SUB] / cnt.astype(F32), mean)
            return (mean - cur.astype(F32)).astype(BF16)

        at_start = (st == 0) if s == 0 else None
        at_end = (st == seq_tiles - 1) if s == nsub - 1 else None
        if at_start is None and at_end is None:
            diff = interior()
        else:
            edge = at_start if at_end is None else (at_end if at_start is None else at_start | at_end)
            diff = lax.cond(edge, clipped, interior)
        o_ref[s * POOL_SUB:(s + 1) * POOL_SUB] = (_dot(diff, pw_ref[...]) * ps_ref[...]).astype(BF16)


def _pool(up, pool_w_bd, pool_scale, seq_len, tm):
    n = up.shape[0]
    hb = tm // POOL_HALO
    n_halo_blocks = n // POOL_HALO
    bands, inner = _pool_bands()
    return pl.pallas_call(
        functools.partial(_pool_kernel, seq_tiles=seq_len // tm, nsub=tm // POOL_SUB),
        grid=(n // tm,),
        in_specs=[_resident(bands.shape, lambda i: (0, 0)),
                  _resident(inner.shape, lambda i: (0, 0, 0)),
                  pl.BlockSpec((POOL_HALO, POOL_WIDTH), lambda i: (jnp.maximum(i * hb - 1, 0), 0)),
                  pl.BlockSpec((tm, POOL_WIDTH), lambda i: (i, 0)),
                  pl.BlockSpec((POOL_HALO, POOL_WIDTH),
                               lambda i: (jnp.minimum((i + 1) * hb, n_halo_blocks - 1), 0)),
                  _resident((POOL_WIDTH, POOL_WIDTH), lambda i: (0, 0)),
                  _resident((1, POOL_WIDTH), lambda i: (0, 0))],
        out_specs=pl.BlockSpec((tm, POOL_WIDTH), lambda i: (i, 0)),
        out_shape=jax.ShapeDtypeStruct((n, POOL_WIDTH), BF16),
        compiler_params=_params(1),
        name="pool",
    )(bands, inner, up, up, up, pool_w_bd, pool_scale)


FFT_CHUNK = 32


def _fft1_kernel(m_ref, x_ref, y_ref):
    xt = jnp.transpose(x_ref[...], (1, 0, 2))
    y = jnp.stack([_dot(m_ref[...], xt[n1]).astype(BF16) for n1 in range(FFT_CHUNK)], axis=0)
    y_ref[...] = jnp.transpose(y, (1, 0, 2)).reshape(y_ref.shape)


def _fourier_epilogue(zz, cs_ref, w_ref, norm):
    f = (_dot(zz, cs_ref[...]) * norm).astype(BF16)
    return _dot(f, w_ref[...])


def _fft2_kernel(t_ref, y_ref, cs_ref, w_ref, o_ref, zz, *, norm):
    for kk in range(FFT_CHUNK):
        z = _dot(t_ref[kk], y_ref[kk].reshape(2 * FFT_N1, FOURIER_WIDTH))
        zz[kk * FFT_N1:(kk + 1) * FFT_N1, :] = jnp.concatenate([z[:FFT_N1], z[FFT_N1:]], axis=1).astype(BF16)
    o = _fourier_epilogue(zz[...], cs_ref, w_ref, norm).astype(BF16)
    o_ref[...] = jnp.transpose(o.reshape(FFT_CHUNK, FFT_N1, FOURIER_WIDTH), (1, 0, 2))


def _fourier_latent(uf, fourier_w_bd):
    stage1, stage2, _, chan = _dft_constants()
    chunks1 = FFT_N1 // FFT_CHUNK
    y = pl.pallas_call(
        _fft1_kernel,
        grid=(BATCH, chunks1),
        in_specs=[_resident((2 * FFT_N2, FFT_N2), lambda b, j: (0, 0)),
                  pl.BlockSpec((FFT_N2, FFT_CHUNK, FOURIER_WIDTH), lambda b, j: (b, j, 0))],
        out_specs=pl.BlockSpec((FFT_N2, 2, FFT_CHUNK, FOURIER_WIDTH), lambda b, j: (b, 0, j, 0)),
        out_shape=jax.ShapeDtypeStruct((BATCH * FFT_N2, 2, FFT_N1, FOURIER_WIDTH), BF16),
        compiler_params=_params(2),
        name="fft_stage1",
    )(stage1, uf.reshape(BATCH * FFT_N2, FFT_N1, FOURIER_WIDTH))
    chunks2 = FFT_N2 // FFT_CHUNK
    out = pl.pallas_call(
        functools.partial(_fft2_kernel, norm=float((SEQ * GROUP_WIDTH) ** -0.5)),
        grid=(chunks2, BATCH),
        in_specs=[pl.BlockSpec((FFT_CHUNK, 2 * FFT_N1, 2 * FFT_N1), lambda j, b: (j, 0, 0)),
                  pl.BlockSpec((FFT_CHUNK, 2, FFT_N1, FOURIER_WIDTH), lambda j, b: (b * chunks2 + j, 0, 0, 0)),
                  _resident((2 * FOURIER_WIDTH, FOURIER_WIDTH), lambda j, b: (0, 0)),
                  _resident((FOURIER_WIDTH, FOURIER_WIDTH), lambda j, b: (0, 0))],
        out_specs=pl.BlockSpec((FFT_N1, FFT_CHUNK, FOURIER_WIDTH), lambda j, b: (b, j, 0)),
        out_shape=jax.ShapeDtypeStruct((BATCH * FFT_N1, FFT_N2, FOURIER_WIDTH), BF16),
        scratch_shapes=[pltpu.VMEM((FFT_CHUNK * FFT_N1, 2 * FOURIER_WIDTH), BF16)],
        compiler_params=_params(2),
        name="fft_stage2",
    )(stage2, y, chan, fourier_w_bd)
    return out.reshape(BATCH * SEQ, FOURIER_WIDTH)


def _fourier_ctx_kernel(m_ref, u_ref, cs_ref, w_ref, o_ref, *, norm):
    z = _dot(m_ref[...], u_ref[...])
    zz = jnp.concatenate([z[:CTX_LEN], z[CTX_LEN:]], axis=1).astype(BF16)
    o_ref[...] = _fourier_epilogue(zz, cs_ref, w_ref, norm).astype(BF16)


def _fourier_ctx(ufc, fourier_w_bd):
    _, _, ctx_dft, chan = _dft_constants()
    return pl.pallas_call(
        functools.partial(_fourier_ctx_kernel, norm=float((CTX_LEN * GROUP_WIDTH) ** -0.5)),
        grid=(BATCH,),
        in_specs=[_resident((2 * CTX_LEN, CTX_LEN), lambda b: (0, 0)),
                  pl.BlockSpec((CTX_LEN, FOURIER_WIDTH), lambda b: (b, 0)),
                  _resident((2 * FOURIER_WIDTH, FOURIER_WIDTH), lambda b: (0, 0)),
                  _resident((FOURIER_WIDTH, FOURIER_WIDTH), lambda b: (0, 0))],
        out_specs=pl.BlockSpec((CTX_LEN, FOURIER_WIDTH), lambda b: (b, 0)),
        out_shape=jax.ShapeDtypeStruct((BATCH * CTX_LEN, FOURIER_WIDTH), BF16),
        compiler_params=_params(1),
        name="fourier_ctx",
    )(ctx_dft, ufc, chan, fourier_w_bd)


def _split_pair(x, even):
    zero = jnp.zeros_like(x)
    return jnp.concatenate([jnp.where(even, x, zero), jnp.where(even, zero, x)], axis=0)


def _softmax_pv(scores, values, n_q, even):
    m = functools.reduce(jnp.maximum, [jnp.max(s, axis=1, keepdims=True) for s in scores])
    e = [jnp.exp2(s - m) for s in scores]
    l = functools.reduce(jnp.add, [jnp.sum(x, axis=1, keepdims=True) for x in e])
    o = functools.reduce(jnp.add, [_dot(x.astype(BF16), v) for x, v in zip(e, values)])
    o = o / l
    return jnp.where(even, o[:n_q], o[n_q:])


ATTN_TILE_ROWS = 32
ATTN_TILE = ATTN_TILE_ROWS * GRID_W
ATTN_HALO = WIN_R * GRID_W
ATTN_ROWS_PER_TRIP = 4
SOFTMAX_ROWS = 32


def _kv_window_row(t):
    return jnp.clip(t * ATTN_TILE_ROWS - WIN_R, 0, GRID_ROWS - ATTN_TILE_ROWS - 2 * WIN_R)


def _nat_kernel(q_ref, kbuf, vbuf, kx_ref, vx_ref, bias_ref, o_ref, sbuf_a, sbuf_b, pbuf):
    t = pl.program_id(1)
    even = lax.broadcasted_iota(jnp.int32, (GRID_W, PAIR_WIDTH), 1) < HEAD_DIM
    n_win = WIN_R * GRID_W

    def window_offset(j):
        r = t * ATTN_TILE_ROWS + j
        rs = jnp.clip(r - WIN_R // 2, 0, GRID_ROWS - WIN_R)
        return pl.multiple_of((rs - _kv_window_row(t)) * GRID_W, GRID_W)

    def query_offset(j):
        return j * GRID_W if isinstance(j, int) else pl.multiple_of(j * GRID_W, GRID_W)

    def scores(j, sbuf):
        r = t * ATTN_TILE_ROWS + j
        off = window_offset(j)
        dr0 = jnp.clip(r - WIN_R // 2, 0, GRID_ROWS - WIN_R) - r + WIN_R - 1
        qoff = query_offset(j)
        for p in range(N_PAIRS):
            lanes = slice(PAIR_WIDTH * p, PAIR_WIDTH * (p + 1))
            qs = _split_pair(q_ref[pl.ds(qoff, GRID_W), lanes], even)
            bias = jnp.concatenate([bias_ref[p, dr0 + 2 * m] for m in range(WIN_R // 2)], axis=1)
            sbuf[p, :, 0:n_win] = _dot_nt(qs, kbuf[pl.ds(off, n_win), lanes]) + bias
            sbuf[p, :, n_win:] = _dot_nt(qs, kx_ref[:, lanes])

    def outputs(j, sbuf):
        off = window_offset(j)
        qoff = query_offset(j)
        for p in range(N_PAIRS):
            lanes = slice(PAIR_WIDTH * p, PAIR_WIDTH * (p + 1))
            sums = []
            for g in range(2 * GRID_W // SOFTMAX_ROWS):
                rows = slice(g * SOFTMAX_ROWS, (g + 1) * SOFTMAX_ROWS)
                s = sbuf[p, rows, :]
                e = jnp.exp2(s - jnp.max(s, axis=1, keepdims=True))
                sums.append(jnp.sum(e, axis=1, keepdims=True))
                pbuf[p, rows, :] = e.astype(BF16)
            l = jnp.concatenate(sums, axis=0)
            o = (_dot(pbuf[p, :, :n_win], vbuf[pl.ds(off, n_win), lanes])
                 + _dot(pbuf[p, :, n_win:], vx_ref[:, lanes]))
            o = o / l
            o_ref[pl.ds(qoff, GRID_W), lanes] = jnp.where(even, o[:GRID_W], o[GRID_W:]).astype(BF16)

    scores(0, sbuf_a)

    def body(i, carry):
        base = ATTN_ROWS_PER_TRIP * i
        for k in range(ATTN_ROWS_PER_TRIP):
            cur, nxt = (sbuf_a, sbuf_b) if k % 2 == 0 else (sbuf_b, sbuf_a)
            scores(jnp.minimum(base + k + 1, ATTN_TILE_ROWS - 1), nxt)
            outputs(base + k, cur)
        return carry

    lax.fori_loop(0, ATTN_TILE_ROWS // ATTN_ROWS_PER_TRIP, body, 0)


def _neighbourhood_attention(qkv, qkv_ctx, bias):
    tiles = SEQ // ATTN_TILE
    cur = lambda col: (lambda b, t: (b * tiles + t, col))
    blk = lambda f: pl.BlockSpec((ATTN_TILE, ATTN_WIDTH), f)
    kv = lambda col: pl.BlockSpec(
        (pl.Element(ATTN_TILE + 2 * ATTN_HALO), pl.Element(ATTN_WIDTH)),
        lambda b, t: (pl.multiple_of(b * SEQ + _kv_window_row(t) * GRID_W, GRID_W), col * ATTN_WIDTH))
    return pl.pallas_call(
        _nat_kernel,
        grid=(BATCH, tiles),
        in_specs=[blk(cur(0)), kv(1), kv(2),
                  pl.BlockSpec((CTX_LEN, ATTN_WIDTH), lambda b, t: (b, 1)),
                  pl.BlockSpec((CTX_LEN, ATTN_WIDTH), lambda b, t: (b, 2)),
                  _resident(bias.shape, lambda b, t: (0, 0, 0, 0))],
        out_specs=blk(cur(0)),
        out_shape=jax.ShapeDtypeStruct((BATCH * SEQ, ATTN_WIDTH), BF16),
        scratch_shapes=[pltpu.VMEM((N_PAIRS, 2 * GRID_W, WIN_R * GRID_W + CTX_LEN), F32),
                        pltpu.VMEM((N_PAIRS, 2 * GRID_W, WIN_R * GRID_W + CTX_LEN), F32),
                        pltpu.VMEM((N_PAIRS, 2 * GRID_W, WIN_R * GRID_W + CTX_LEN), BF16)],
        compiler_params=_params(2),
        name="nat_attention",
    )(qkv, qkv, qkv, qkv_ctx, qkv_ctx, bias)


def _ctx_attn_kernel(q_ref, k_ref, v_ref, o_ref):
    even = lax.broadcasted_iota(jnp.int32, (CTX_LEN, PAIR_WIDTH), 1) < HEAD_DIM
    for p in range(N_PAIRS):
        lanes = slice(PAIR_WIDTH * p, PAIR_WIDTH * (p + 1))
        qs = _split_pair(q_ref[:, lanes], even)
        s = _dot_nt(qs, k_ref[:, lanes])
        o_ref[:, lanes] = _softmax_pv([s], [v_ref[:, lanes]], CTX_LEN, even).astype(BF16)


def _context_attention(qkv_ctx):
    blk = lambda col: pl.BlockSpec((CTX_LEN, ATTN_WIDTH), lambda b: (b, col))
    return pl.pallas_call(
        _ctx_attn_kernel,
        grid=(BATCH,),
        in_specs=[blk(0), blk(1), blk(2)],
        out_specs=blk(0),
        out_shape=jax.ShapeDtypeStruct((BATCH * CTX_LEN, ATTN_WIDTH), BF16),
        compiler_params=_params(1),
        name="ctx_attention",
    )(qkv_ctx, qkv_ctx, qkv_ctx)


V7X_MXU_WIDTH = 256
FFN_CHUNKS = ((0, 6 * V7X_MXU_WIDTH), (6 * V7X_MXU_WIDTH, D_FF))


def _outffn_kernel(x_ref, yp_ref, yf_ref, at_ref, mod_ref, wo_ref, g2_ref, w1_ref, w3_ref, w2_ref,
                   *rest, final):
    o_ref = rest[-1]
    y = jnp.concatenate([yp_ref[...], yf_ref[...], at_ref[...]], axis=1)
    x1 = x_ref[...] + mod_ref[0, 2:3, :] * _dot(y, wo_ref[...])
    h = (_rms(x1) * g2_ref[...] * (1.0 + mod_ref[0, 4:5, :]) + mod_ref[0, 3:4, :]).astype(BF16)
    ffn = jnp.zeros_like(x1)
    for lo, hi in FFN_CHUNKS:
        cols = slice(lo, hi)
        a = _dot(h, w1_ref[:, cols])
        gated = (a * jax.nn.sigmoid(a) * _dot(h, w3_ref[:, cols])).astype(BF16)
        ffn = ffn + _dot(gated, w2_ref[cols, :])
    x2 = x1 + mod_ref[0, 5:6, :] * ffn
    if final:
        x2 = _rms(x2) * rest[0][...]
    o_ref[...] = x2


def _outffn(x, yp, yf, at, mod, w_out, g2, w1, w3, w2, final_g, mod_row, tm):
    n = x.shape[0]
    row = lambda w: pl.BlockSpec((tm, w), lambda i: (i, 0))
    whole = lambda w: _resident(w.shape, lambda i: (0, 0))
    in_specs = [row(D_MODEL), row(POOL_WIDTH), row(FOURIER_WIDTH), row(ATTN_WIDTH),
                pl.BlockSpec((1, 6, D_MODEL), lambda i: (mod_row(i, tm), 0, 0)),
                whole(w_out), whole(g2), whole(w1), whole(w3), whole(w2)]
    args = [x, yp, yf, at, mod, w_out, g2, w1, w3, w2]
    if final_g is not None:
        in_specs.append(_resident((1, D_MODEL), lambda i: (0, 0)))
        args.append(final_g)
    return pl.pallas_call(
        functools.partial(_outffn_kernel, final=final_g is not None),
        grid=(n // tm,),
        in_specs=in_specs,
        out_specs=row(D_MODEL),
        out_shape=jax.ShapeDtypeStruct((n, D_MODEL), F32),
        compiler_params=_params(1),
        name="outproj_ffn",
    )(*args)


def _latent_mod_row(i, tm):
    return (i * tm) // SEQ


def _ctx_mod_row(i, tm):
    return BATCH


def kernel(x, c, ctx, c_ctx, w_ada, b_ada, norm1_g, w_in, pool_w, pool_scale, fourier_w, nat_bias, w_out,
           norm2_g, w_ffn1, w_ffn3, w_ffn2, final_g):
    cond = jnp.concatenate([c, c_ctx[None, :], jnp.zeros((8 - BATCH - 1, D_MODEL), F32)], axis=0)
    mods = _ada_mod(cond, w_ada, b_ada)
    x = x.reshape(BATCH * SEQ, D_MODEL)
    ctx = ctx.reshape(BATCH * CTX_LEN, D_MODEL)
    w_in = w_in.astype(BF16)
    for l in range(DEPTH):
        last = l == DEPTH - 1
        mod = mods[l]
        g1 = norm1_g[l][None, :]
        g2 = norm2_g[l][None, :]
        pool_w_bd = _group_block_diag(pool_w[l])
        fourier_w_bd = _group_block_diag(fourier_w[l])
        scale = pool_scale[l][None, :]

        upc, ufc, qkvc = _inproj(ctx, mod, g1, w_in, l, _ctx_mod_row, 512)
        up, uf, qkv, wo_l, w1_l, w3_l, w2_l = _inproj(x, mod, g1, w_in, l, _latent_mod_row, 1024,
                                                      cast=(w_out, w_ffn1, w_ffn3, w_ffn2))
        ffn_w = (wo_l, g2, w1_l, w3_l, w2_l)
        yp = _pool(up, pool_w_bd, scale, SEQ, SEQ)
        yf = _fourier_latent(uf, fourier_w_bd)
        at = _neighbourhood_attention(qkv, qkvc, _attention_bias(nat_bias[l]))
        x = _outffn(x, yp, yf, at, mod, *ffn_w, final_g[None, :] if last else None, _latent_mod_row, 1024)
        if not last:
            ypc = _pool(upc, pool_w_bd, scale, CTX_LEN, CTX_LEN)
            yfc = _fourier_ctx(ufc, fourier_w_bd)
            atc = _context_attention(qkvc)
            ctx = _outffn(ctx, ypc, yfc, atc, mod, *ffn_w, None, _ctx_mod_row, 512)
    return x.reshape(BATCH, SEQ, D_MODEL)
```

```python
import functools

import numpy as np
import jax
import jax.numpy as jnp
from jax import lax
from jax.experimental import pallas as pl
from jax.experimental.pallas import tpu as pltpu

D_MODEL = 1024
BATCH = 4
SEQ = 8192
DEPTH = 4
GRID_W = 64
GRID_ROWS = SEQ // GRID_W
CTX_LEN = 256
POOL_WINDOWS = (2, 4, 8, 16)
POOL_WIDTH = 256
FOURIER_WIDTH = 256
GROUP_WIDTH = 64
HEAD_DIM = 64
N_HEADS = 8
N_PAIRS = N_HEADS // 2
ATTN_WIDTH = 512
WIN_R = 8
WIN_C = 16
D_FF = 2816
EPS = 1e-6
NEG_INF = -1e30
LOG2_E = 1.4426950408889634
Q_SCALE = HEAD_DIM ** -0.5 * LOG2_E

FFT_N1 = 128
FFT_N2 = 64
POOL_HALO = 16
BIAS_SHAPE = (N_PAIRS, 2 * WIN_R - 2, 2 * GRID_W, 2 * GRID_W)

V7X_VMEM_LIMIT = 56 * 1024 * 1024
V7X_LANES = 128
PAIR_WIDTH = 2 * HEAD_DIM

F32 = jnp.float32
BF16 = jnp.bfloat16


def _dot(a, b):
    return jnp.dot(a, b, preferred_element_type=F32)


def _dot_nt(a, b):
    return lax.dot_general(a, b, (((1,), (1,)), ((), ())), preferred_element_type=F32)


def _params(n_grid_dims):
    return pltpu.CompilerParams(dimension_semantics=("arbitrary",) * n_grid_dims,
                                vmem_limit_bytes=V7X_VMEM_LIMIT)


def _resident(block_shape, index_map):
    return pl.BlockSpec(block_shape, index_map, pipeline_mode=pl.Buffered(1))


def _block_diag(m, groups):
    w = m.shape[0]
    out = np.zeros((groups * w, groups * w), m.dtype)
    for g in range(groups):
        out[g * w:(g + 1) * w, g * w:(g + 1) * w] = m
    return out


@functools.lru_cache(maxsize=None)
def _dft_constants():
    two_pi = 2.0 * np.pi
    a2 = two_pi * (np.outer(np.arange(FFT_N2), np.arange(FFT_N2)) % FFT_N2) / FFT_N2
    stage1 = np.stack([np.cos(a2), -np.sin(a2)], axis=1).reshape(2 * FFT_N2, FFT_N2)
    k = FFT_N2 * np.arange(FFT_N1)[None, :, None] + np.arange(FFT_N2)[:, None, None]
    n1 = np.arange(FFT_N1)[None, None, :]
    ang = two_pi * ((k * n1) % SEQ) / SEQ
    mr, mi = np.cos(ang), -np.sin(ang)
    stage2 = np.concatenate([np.concatenate([mr, -mi], axis=2),
                             np.concatenate([mi, mr], axis=2)], axis=1)
    ac = two_pi * (np.outer(np.arange(CTX_LEN), np.arange(CTX_LEN)) % CTX_LEN) / CTX_LEN
    ctx_dft = np.concatenate([np.cos(ac), -np.sin(ac)], axis=0)
    ag = two_pi * (np.outer(np.arange(GROUP_WIDTH), np.arange(GROUP_WIDTH)) % GROUP_WIDTH) / GROUP_WIDTH
    chan = np.concatenate([_block_diag(np.cos(ag), 4), _block_diag(np.sin(ag), 4)], axis=0)
    as_bf16 = lambda a: jnp.asarray(a, dtype=F32).astype(BF16)
    return as_bf16(stage1), as_bf16(stage2), as_bf16(ctx_dft), as_bf16(chan)


def _group_block_diag(w):
    out = jnp.zeros((4, GROUP_WIDTH, 4, GROUP_WIDTH), w.dtype)
    for g in range(4):
        out = out.at[g, :, g, :].set(w[g])
    return out.reshape(4 * GROUP_WIDTH, 4 * GROUP_WIDTH).astype(BF16)


def _attention_bias(table):
    col = np.arange(GRID_W)
    cs = np.clip(col - WIN_C // 2, 0, GRID_W - WIN_C)
    col_mask = (col[None, :] >= cs[:, None]) & (col[None, :] < cs[:, None] + WIN_C)
    dc_idx = np.clip(col[None, :] - col[:, None] + WIN_C - 1, 0, 2 * WIN_C - 2)
    onehot = (dc_idx[None] == np.arange(2 * WIN_C - 1)[:, None, None]).astype(np.float32)
    bt = jnp.einsum('hrd,dcj->hrcj', table.astype(F32), jnp.asarray(onehot),
                    precision=lax.Precision.HIGHEST)
    bt = jnp.where(col_mask[None, None], bt * LOG2_E, NEG_INF)
    b = jnp.concatenate([bt[:, :-1], bt[:, 1:]], axis=-1)
    b = b.reshape(N_PAIRS, 2, 2 * WIN_R - 2, GRID_W, 2 * GRID_W).transpose(0, 2, 1, 3, 4)
    return b.reshape(BIAS_SHAPE)


def _ada_kernel(c_ref, w_ref, b_ref, o_ref):
    c = c_ref[...]
    a = (c * jax.nn.sigmoid(c)).astype(BF16)
    o_ref[0] = _dot(a, w_ref[0].astype(BF16)) + b_ref[0]


def _ada_mod(cond, w_ada, b_ada):
    tn = 1536
    out = pl.pallas_call(
        _ada_kernel,
        grid=(DEPTH, 6 * D_MODEL // tn),
        in_specs=[_resident((8, D_MODEL), lambda l, j: (0, 0)),
                  pl.BlockSpec((1, D_MODEL, tn), lambda l, j: (l, 0, j)),
                  pl.BlockSpec((1, 1, tn), lambda l, j: (l, 0, j))],
        out_specs=pl.BlockSpec((1, 8, tn), lambda l, j: (l, 0, j)),
        out_shape=jax.ShapeDtypeStruct((DEPTH, 8, 6 * D_MODEL), F32),
        compiler_params=_params(2),
        name="ada_mod",
    )(cond, w_ada, b_ada.reshape(DEPTH, 1, 6 * D_MODEL))
    return out.reshape(DEPTH, 8, 6, D_MODEL)


def _rms(x):
    return x * lax.rsqrt(jnp.mean(x * x, axis=-1, keepdims=True) + EPS)


P_END = POOL_WIDTH
F_END = P_END + FOURIER_WIDTH
Q_END = F_END + ATTN_WIDTH
K_END = Q_END + ATTN_WIDTH
IN_WIDTH = K_END + ATTN_WIDTH
QKV_WIDTH = 3 * ATTN_WIDTH
INPROJ_PARTS = 4


def _inproj_kernel(x_ref, mod_ref, g_ref, w_ref, *refs):
    n_cast = (len(refs) - 3) // 2
    up_ref, uf_ref, qkv_ref = refs[n_cast:n_cast + 3]
    for src, dst in zip(refs[:n_cast], refs[n_cast + 3:]):
        dst[...] = src[...].astype(BF16)
    part_rows = x_ref.shape[0] // INPROJ_PARTS
    for part in range(INPROJ_PARTS):
        rows = slice(part * part_rows, (part + 1) * part_rows)
        y = _rms(x_ref[rows, :]) * g_ref[...]
        h = (y * (1.0 + mod_ref[0, 1:2, :]) + mod_ref[0, 0:1, :]).astype(BF16)
        up_ref[rows, :] = _dot(h, w_ref[:, 0:P_END]).astype(BF16)
        uf_ref[rows, :] = _dot(h, w_ref[:, P_END:F_END]).astype(BF16)
        qkv_ref[rows, 0:ATTN_WIDTH] = (_dot(h, w_ref[:, F_END:Q_END]) * Q_SCALE).astype(BF16)
        qkv_ref[rows, ATTN_WIDTH:2 * ATTN_WIDTH] = _dot(h, w_ref[:, Q_END:K_END]).astype(BF16)
        qkv_ref[rows, 2 * ATTN_WIDTH:QKV_WIDTH] = _dot(h, w_ref[:, K_END:IN_WIDTH]).astype(BF16)


def _layer_weight(layer, rows, cols):
    return _resident((None, rows, cols), lambda i: (layer, 0, 0))


BF16_SUBLANES = 16


def _inproj(x, mod, g, w_in, layer, mod_row, tm, cast=()):
    n = x.shape[0]
    steps = n // tm
    cast_in, cast_out, cast_shape = [], [], []
    for w in cast:
        _, rows, cols = w.shape
        share = next(k for k in (1, 2, 4, 8) if (rows * k) % (steps * BF16_SUBLANES) == 0)
        slab = rows * share // steps
        cast_in.append(pl.BlockSpec((None, slab, cols), lambda i, share=share: (layer, i // share, 0)))
        cast_out.append(pl.BlockSpec((slab, cols), lambda i, share=share: (i // share, 0)))
        cast_shape.append(jax.ShapeDtypeStruct((rows, cols), BF16))
    return pl.pallas_call(
        _inproj_kernel,
        grid=(steps,),
        in_specs=[pl.BlockSpec((tm, D_MODEL), lambda i: (i, 0)),
                  pl.BlockSpec((1, 6, D_MODEL), lambda i: (mod_row(i, tm), 0, 0)),
                  _resident((1, D_MODEL), lambda i: (0, 0)),
                  _layer_weight(layer, D_MODEL, IN_WIDTH)] + cast_in,
        out_specs=[pl.BlockSpec((tm, POOL_WIDTH), lambda i: (i, 0)),
                   pl.BlockSpec((tm, FOURIER_WIDTH), lambda i: (i, 0)),
                   pl.BlockSpec((tm, QKV_WIDTH), lambda i: (i, 0))] + cast_out,
        out_shape=[jax.ShapeDtypeStruct((n, POOL_WIDTH), BF16),
                   jax.ShapeDtypeStruct((n, FOURIER_WIDTH), BF16),
                   jax.ShapeDtypeStruct((n, QKV_WIDTH), BF16)] + cast_shape,
        compiler_params=_params(1),
        name="inproj",
    )(x, mod, g, w_in, *cast)


POOL_SUB = 256
POOL_BAND_K = POOL_SUB + 2 * POOL_HALO


@functools.lru_cache(maxsize=None)
def _pool_bands():
    t = np.arange(POOL_SUB)[:, None]
    d = np.arange(POOL_BAND_K)[None, :] - POOL_HALO - t
    member = [((d >= -(w // 2)) & (d < w - w // 2)).astype(np.float64) for w in POOL_WINDOWS]
    centred = [m / w - (d == 0) for m, w in zip(member, POOL_WINDOWS)]
    interior = np.stack([np.concatenate(centred[0:2], axis=1), np.concatenate(centred[2:4], axis=1)], axis=0)
    as_bf16 = lambda a: jnp.asarray(a, dtype=F32).astype(BF16)
    return as_bf16(np.concatenate(member, axis=0)), as_bf16(interior)


def _pool_kernel(band_ref, inner_ref, prev_ref, cur_ref, next_ref, pw_ref, ps_ref, o_ref, *, seq_tiles, nsub):
    st = pl.program_id(0) % seq_tiles
    ub_all = jnp.concatenate([prev_ref[...], cur_ref[...], next_ref[...]], axis=0)
    left = lax.broadcasted_iota(jnp.int32, (POOL_BAND_K, V7X_LANES), 1) < GROUP_WIDTH

    for s in range(nsub):
        ub = ub_all[s * POOL_SUB:s * POOL_SUB + POOL_BAND_K]
        cur = ub[POOL_HALO:POOL_HALO + POOL_SUB]

        def interior(ub=ub):
            halves = []
            for h in range(2):
                t = ub[:, V7X_LANES * h:V7X_LANES * (h + 1)]
                zero = jnp.zeros_like(t)
                stacked = jnp.concatenate([jnp.where(left, t, zero), jnp.where(left, zero, t)], axis=0)
                halves.append(_dot(inner_ref[h], stacked))
            return jnp.concatenate(halves, axis=1).astype(BF16)

        def clipped(ub=ub, cur=cur, s=s):
            lo_b = jnp.where(st == 0, 0, -POOL_HALO) if s == 0 else -POOL_HALO
            hi_b = (jnp.where(st == seq_tiles - 1, POOL_SUB, POOL_SUB + POOL_HALO) if s == nsub - 1
                    else POOL_SUB + POOL_HALO)
            p = lax.broadcasted_iota(jnp.int32, ub.shape, 0) - POOL_HALO
            ubv = jnp.where((p >= lo_b) & (p < hi_b), ub, jnp.zeros_like(ub))
            sums = _dot(band_ref[...], ubv)
            tcol = lax.broadcasted_iota(jnp.int32, (POOL_SUB, 1), 0)
            group = lax.broadcasted_iota(jnp.int32, (POOL_SUB, POOL_WIDTH), 1) // GROUP_WIDTH
            mean = jnp.zeros((POOL_SUB, POOL_WIDTH), F32)
            for g, w in enumerate(POOL_WINDOWS):
                cnt = jnp.minimum(tcol + (w - w // 2), hi_b) - jnp.maximum(tcol - w // 2, lo_b)
                mean = jnp.where(group == g, sums[g * POOL_SUB:(g + 1) * POOL_SUB] / cnt.astype(F32), mean)
            return (mean - cur.astype(F32)).astype(BF16)

        at_start = (st == 0) if s == 0 else None
        at_end = (st == seq_tiles - 1) if s == nsub - 1 else None
        if at_start is None and at_end is None:
            diff = interior()
        else:
            edge = at_start if at_end is None else (at_end if at_start is None else at_start | at_end)
            diff = lax.cond(edge, clipped, interior)
        o_ref[s * POOL_SUB:(s + 1) * POOL_SUB] = (_dot(diff, pw_ref[...]) * ps_ref[...]).astype(BF16)


def _pool(up, pool_w_bd, pool_scale, seq_len, tm):
    n = up.shape[0]
    hb = tm // POOL_HALO
    n_halo_blocks = n // POOL_HALO
    bands, inner = _pool_bands()
    return pl.pallas_call(
        functools.partial(_pool_kernel, seq_tiles=seq_len // tm, nsub=tm // POOL_SUB),
        grid=(n // tm,),
        in_specs=[_resident(bands.shape, lambda i: (0, 0)),
                  _resident(inner.shape, lambda i: (0, 0, 0)),
                  pl.BlockSpec((POOL_HALO, POOL_WIDTH), lambda i: (jnp.maximum(i * hb - 1, 0), 0)),
                  pl.BlockSpec((tm, POOL_WIDTH), lambda i: (i, 0)),
                  pl.BlockSpec((POOL_HALO, POOL_WIDTH),
                               lambda i: (jnp.minimum((i + 1) * hb, n_halo_blocks - 1), 0)),
                  _resident((POOL_WIDTH, POOL_WIDTH), lambda i: (0, 0)),
                  _resident((1, POOL_WIDTH), lambda i: (0, 0))],
        out_specs=pl.BlockSpec((tm, POOL_WIDTH), lambda i: (i, 0)),
        out_shape=jax.ShapeDtypeStruct((n, POOL_WIDTH), BF16),
        compiler_params=_params(1),
        name="pool",
    )(bands, inner, up, up, up, pool_w_bd, pool_scale)


FFT_CHUNK = 32


def _fft1_kernel(m_ref, x_ref, y_ref):
    xt = jnp.transpose(x_ref[...], (1, 0, 2))
    y = jnp.stack([_dot(m_ref[...], xt[n1]).astype(BF16) for n1 in range(FFT_CHUNK)], axis=0)
    y_ref[...] = jnp.transpose(y, (1, 0, 2)).reshape(y_ref.shape)


def _fourier_epilogue(zz, cs_ref, w_ref, norm):
    f = (_dot(zz, cs_ref[...]) * norm).astype(BF16)
    return _dot(f, w_ref[...])


def _fft2_kernel(t_ref, y_ref, cs_ref, w_ref, o_ref, zz, *, norm):
    for kk in range(FFT_CHUNK):
        z = _dot(t_ref[kk], y_ref[kk].reshape(2 * FFT_N1, FOURIER_WIDTH))
        zz[kk * FFT_N1:(kk + 1) * FFT_N1, :] = jnp.concatenate([z[:FFT_N1], z[FFT_N1:]], axis=1).astype(BF16)
    o = _fourier_epilogue(zz[...], cs_ref, w_ref, norm).astype(BF16)
    o_ref[...] = jnp.transpose(o.reshape(FFT_CHUNK, FFT_N1, FOURIER_WIDTH), (1, 0, 2))


def _fourier_latent(uf, fourier_w_bd):
    stage1, stage2, _, chan = _dft_constants()
    chunks1 = FFT_N1 // FFT_CHUNK
    y = pl.pallas_call(
        _fft1_kernel,
        grid=(BATCH, chunks1),
        in_specs=[_resident((2 * FFT_N2, FFT_N2), lambda b, j: (0, 0)),
                  pl.BlockSpec((FFT_N2, FFT_CHUNK, FOURIER_WIDTH), lambda b, j: (b, j, 0))],
        out_specs=pl.BlockSpec((FFT_N2, 2, FFT_CHUNK, FOURIER_WIDTH), lambda b, j: (b, 0, j, 0)),
        out_shape=jax.ShapeDtypeStruct((BATCH * FFT_N2, 2, FFT_N1, FOURIER_WIDTH), BF16),
        compiler_params=_params(2),
        name="fft_stage1",
    )(stage1, uf.reshape(BATCH * FFT_N2, FFT_N1, FOURIER_WIDTH))
    chunks2 = FFT_N2 // FFT_CHUNK
    out = pl.pallas_call(
        functools.partial(_fft2_kernel, norm=float((SEQ * GROUP_WIDTH) ** -0.5)),
        grid=(chunks2, BATCH),
        in_specs=[pl.BlockSpec((FFT_CHUNK, 2 * FFT_N1, 2 * FFT_N1), lambda j, b: (j, 0, 0)),
                  pl.BlockSpec((FFT_CHUNK, 2, FFT_N1, FOURIER_WIDTH), lambda j, b: (b * chunks2 + j, 0, 0, 0)),
                  _resident((2 * FOURIER_WIDTH, FOURIER_WIDTH), lambda j, b: (0, 0)),
                  _resident((FOURIER_WIDTH, FOURIER_WIDTH), lambda j, b: (0, 0))],
        out_specs=pl.BlockSpec((FFT_N1, FFT_CHUNK, FOURIER_WIDTH), lambda j, b: (b, j, 0)),
        out_shape=jax.ShapeDtypeStruct((BATCH * FFT_N1, FFT_N2, FOURIER_WIDTH), BF16),
        scratch_shapes=[pltpu.VMEM((FFT_CHUNK * FFT_N1, 2 * FOURIER_WIDTH), BF16)],
        compiler_params=_params(2),
        name="fft_stage2",
    )(stage2, y, chan, fourier_w_bd)
    return out.reshape(BATCH * SEQ, FOURIER_WIDTH)


def _fourier_ctx_kernel(m_ref, u_ref, cs_ref, w_ref, o_ref, *, norm):
    z = _dot(m_ref[...], u_ref[...])
    zz = jnp.concatenate([z[:CTX_LEN], z[CTX_LEN:]], axis=1).astype(BF16)
    o_ref[...] = _fourier_epilogue(zz, cs_ref, w_ref, norm).astype(BF16)


def _fourier_ctx(ufc, fourier_w_bd):
    _, _, ctx_dft, chan = _dft_constants()
    return pl.pallas_call(
        functools.partial(_fourier_ctx_kernel, norm=float((CTX_LEN * GROUP_WIDTH) ** -0.5)),
        grid=(BATCH,),
        in_specs=[_resident((2 * CTX_LEN, CTX_LEN), lambda b: (0, 0)),
                  pl.BlockSpec((CTX_LEN, FOURIER_WIDTH), lambda b: (b, 0)),
                  _resident((2 * FOURIER_WIDTH, FOURIER_WIDTH), lambda b: (0, 0)),
                  _resident((FOURIER_WIDTH, FOURIER_WIDTH), lambda b: (0, 0))],
        out_specs=pl.BlockSpec((CTX_LEN, FOURIER_WIDTH), lambda b: (b, 0)),
        out_shape=jax.ShapeDtypeStruct((BATCH * CTX_LEN, FOURIER_WIDTH), BF16),
        compiler_params=_params(1),
        name="fourier_ctx",
    )(ctx_dft, ufc, chan, fourier_w_bd)


def _split_pair(x, even):
    zero = jnp.zeros_like(x)
    return jnp.concatenate([jnp.where(even, x, zero), jnp.where(even, zero, x)], axis=0)


def _softmax_pv(scores, values, n_q, even):
    m = functools.reduce(jnp.maximum, [jnp.max(s, axis=1, keepdims=True) for s in scores])
    e = [jnp.exp2(s - m) for s in scores]
    l = functools.reduce(jnp.add, [jnp.sum(x, axis=1, keepdims=True) for x in e])
    o = functools.reduce(jnp.add, [_dot(x.astype(BF16), v) for x, v in zip(e, values)])
    o = o / l
    return jnp.where(even, o[:n_q], o[n_q:])


ATTN_TILE_ROWS = 32
ATTN_TILE = ATTN_TILE_ROWS * GRID_W
ATTN_HALO = WIN_R * GRID_W
ATTN_ROWS_PER_TRIP = 4
SOFTMAX_ROWS = 32


def _kv_window_row(t):
    return jnp.clip(t * ATTN_TILE_ROWS - WIN_R, 0, GRID_ROWS - ATTN_TILE_ROWS - 2 * WIN_R)


def _nat_kernel(q_ref, kbuf, vbuf, kx_ref, vx_ref, bias_ref, o_ref, sbuf_a, sbuf_b, pbuf):
    t = pl.program_id(1)
    even = lax.broadcasted_iota(jnp.int32, (GRID_W, PAIR_WIDTH), 1) < HEAD_DIM
    n_win = WIN_R * GRID_W

    def window_offset(j):
        r = t * ATTN_TILE_ROWS + j
        rs = jnp.clip(r - WIN_R // 2, 0, GRID_ROWS - WIN_R)
        return pl.multiple_of((rs - _kv_window_row(t)) * GRID_W, GRID_W)

    def query_offset(j):
        return j * GRID_W if isinstance(j, int) else pl.multiple_of(j * GRID_W, GRID_W)

    def scores(j, sbuf):
        r = t * ATTN_TILE_ROWS + j
        off = window_offset(j)
        dr0 = jnp.clip(r - WIN_R // 2, 0, GRID_ROWS - WIN_R) - r + WIN_R - 1
        qoff = query_offset(j)
        for p in range(N_PAIRS):
            lanes = slice(PAIR_WIDTH * p, PAIR_WIDTH * (p + 1))
            qs = _split_pair(q_ref[pl.ds(qoff, GRID_W), lanes], even)
            bias = jnp.concatenate([bias_ref[p, dr0 + 2 * m] for m in range(WIN_R // 2)], axis=1)
            sbuf[p, :, 0:n_win] = _dot_nt(qs, kbuf[pl.ds(off, n_win), lanes]) + bias
            sbuf[p, :, n_win:] = _dot_nt(qs, kx_ref[:, lanes])

    def outputs(j, sbuf):
        off = window_offset(j)
        qoff = query_offset(j)
        for p in range(N_PAIRS):
            lanes = slice(PAIR_WIDTH * p, PAIR_WIDTH * (p + 1))
            sums = []
            for g in range(2 * GRID_W // SOFTMAX_ROWS):
                rows = slice(g * SOFTMAX_ROWS, (g + 1) * SOFTMAX_ROWS)
                s = sbuf[p, rows, :]
                e = jnp.exp2(s - jnp.max(s, axis=1, keepdims=True))
                sums.append(jnp.sum(e, axis=1, keepdims=True))
                pbuf[p, rows, :] = e.astype(BF16)
            l = jnp.concatenate(sums, axis=0)
            o = (_dot(pbuf[p, :, :n_win], vbuf[pl.ds(off, n_win), lanes])
                 + _dot(pbuf[p, :, n_win:], vx_ref[:, lanes]))
            o = o / l
            o_ref[pl.ds(qoff, GRID_W), lanes] = jnp.where(even, o[:GRID_W], o[GRID_W:]).astype(BF16)

    scores(0, sbuf_a)

    def body(i, carry):
        base = ATTN_ROWS_PER_TRIP * i
        for k in range(ATTN_ROWS_PER_TRIP):
            cur, nxt = (sbuf_a, sbuf_b) if k % 2 == 0 else (sbuf_b, sbuf_a)
            scores(jnp.minimum(base + k + 1, ATTN_TILE_ROWS - 1), nxt)
            outputs(base + k, cur)
        return carry

    lax.fori_loop(0, ATTN_TILE_ROWS // ATTN_ROWS_PER_TRIP, body, 0)


def _neighbourhood_attention(qkv, qkv_ctx, bias):
    tiles = SEQ // ATTN_TILE
    cur = lambda col: (lambda b, t: (b * tiles + t, col))
    blk = lambda f: pl.BlockSpec((ATTN_TILE, ATTN_WIDTH), f)
    kv = lambda col: pl.BlockSpec(
        (pl.Element(ATTN_TILE + 2 * ATTN_HALO), pl.Element(ATTN_WIDTH)),
        lambda b, t: (pl.multiple_of(b * SEQ + _kv_window_row(t) * GRID_W, GRID_W), col * ATTN_WIDTH))
    return pl.pallas_call(
        _nat_kernel,
        grid=(BATCH, tiles),
        in_specs=[blk(cur(0)), kv(1), kv(2),
                  pl.BlockSpec((CTX_LEN, ATTN_WIDTH), lambda b, t: (b, 1)),
                  pl.BlockSpec((CTX_LEN, ATTN_WIDTH), lambda b, t: (b, 2)),
                  _resident(bias.shape, lambda b, t: (0, 0, 0, 0))],
        out_specs=blk(cur(0)),
        out_shape=jax.ShapeDtypeStruct((BATCH * SEQ, ATTN_WIDTH), BF16),
        scratch_shapes=[pltpu.VMEM((N_PAIRS, 2 * GRID_W, WIN_R * GRID_W + CTX_LEN), F32),
                        pltpu.VMEM((N_PAIRS, 2 * GRID_W, WIN_R * GRID_W + CTX_LEN), F32),
                        pltpu.VMEM((N_PAIRS, 2 * GRID_W, WIN_R * GRID_W + CTX_LEN), BF16)],
        compiler_params=_params(2),
        name="nat_attention",
    )(qkv, qkv, qkv, qkv_ctx, qkv_ctx, bias)


def _ctx_attn_kernel(q_ref, k_ref, v_ref, o_ref):
    even = lax.broadcasted_iota(jnp.int32, (CTX_LEN, PAIR_WIDTH), 1) < HEAD_DIM
    for p in range(N_PAIRS):
        lanes = slice(PAIR_WIDTH * p, PAIR_WIDTH * (p + 1))
        qs = _split_pair(q_ref[:, lanes], even)
        s = _dot_nt(qs, k_ref[:, lanes])
        o_ref[:, lanes] = _softmax_pv([s], [v_ref[:, lanes]], CTX_LEN, even).astype(BF16)


def _context_attention(qkv_ctx):
    blk = lambda col: pl.BlockSpec((CTX_LEN, ATTN_WIDTH), lambda b: (b, col))
    return pl.pallas_call(
        _ctx_attn_kernel,
        grid=(BATCH,),
        in_specs=[blk(0), blk(1), blk(2)],
        out_specs=blk(0),
        out_shape=jax.ShapeDtypeStruct((BATCH * CTX_LEN, ATTN_WIDTH), BF16),
        compiler_params=_params(1),
        name="ctx_attention",
    )(qkv_ctx, qkv_ctx, qkv_ctx)


V7X_MXU_WIDTH = 256
FFN_CHUNKS = ((0, 6 * V7X_MXU_WIDTH), (6 * V7X_MXU_WIDTH, D_FF))


def _outffn_kernel(x_ref, yp_ref, yf_ref, at_ref, mod_ref, wo_ref, g2_ref, w1_ref, w3_ref, w2_ref,
                   *rest, final):
    o_ref = rest[-1]
    y = jnp.concatenate([yp_ref[...], yf_ref[...], at_ref[...]], axis=1)
    x1 = x_ref[...] + mod_ref[0, 2:3, :] * _dot(y, wo_ref[...])
    h = (_rms(x1) * g2_ref[...] * (1.0 + mod_ref[0, 4:5, :]) + mod_ref[0, 3:4, :]).astype(BF16)
    ffn = jnp.zeros_like(x1)
    for lo, hi in FFN_CHUNKS:
        cols = slice(lo, hi)
        a = _dot(h, w1_ref[:, cols])
        gated = (a * jax.nn.sigmoid(a) * _dot(h, w3_ref[:, cols])).astype(BF16)
        ffn = ffn + _dot(gated, w2_ref[cols, :])
    x2 = x1 + mod_ref[0, 5:6, :] * ffn
    if final:
        x2 = _rms(x2) * rest[0][...]
    o_ref[...] = x2


def _outffn(x, yp, yf, at, mod, w_out, g2, w1, w3, w2, final_g, mod_row, tm):
    n = x.shape[0]
    row = lambda w: pl.BlockSpec((tm, w), lambda i: (i, 0))
    whole = lambda w: _resident(w.shape, lambda i: (0, 0))
    in_specs = [row(D_MODEL), row(POOL_WIDTH), row(FOURIER_WIDTH), row(ATTN_WIDTH),
                pl.BlockSpec((1, 6, D_MODEL), lambda i: (mod_row(i, tm), 0, 0)),
                whole(w_out), whole(g2), whole(w1), whole(w3), whole(w2)]
    args = [x, yp, yf, at, mod, w_out, g2, w1, w3, w2]
    if final_g is not None:
        in_specs.append(_resident((1, D_MODEL), lambda i: (0, 0)))
        args.append(final_g)
    return pl.pallas_call(
        functools.partial(_outffn_kernel, final=final_g is not None),
        grid=(n // tm,),
        in_specs=in_specs,
        out_specs=row(D_MODEL),
        out_shape=jax.ShapeDtypeStruct((n, D_MODEL), F32),
        compiler_params=_params(1),
        name="outproj_ffn",
    )(*args)


def _latent_mod_row(i, tm):
    return (i * tm) // SEQ


def _ctx_mod_row(i, tm):
    return BATCH


def kernel(x, c, ctx, c_ctx, w_ada, b_ada, norm1_g, w_in, pool_w, pool_scale, fourier_w, nat_bias, w_out,
           norm2_g, w_ffn1, w_ffn3, w_ffn2, final_g):
    cond = jnp.concatenate([c, c_ctx[None, :], jnp.zeros((8 - BATCH - 1, D_MODEL), F32)], axis=0)
    mods = _ada_mod(cond, w_ada, b_ada)
    x = x.reshape(BATCH * SEQ, D_MODEL)
    ctx = ctx.reshape(BATCH * CTX_LEN, D_MODEL)
    w_in = w_in.astype(BF16)
    for l in range(DEPTH):
        last = l == DEPTH - 1
        mod = mods[l]
        g1 = norm1_g[l][None, :]
        g2 = norm2_g[l][None, :]
        pool_w_bd = _group_block_diag(pool_w[l])
        fourier_w_bd = _group_block_diag(fourier_w[l])
        scale = pool_scale[l][None, :]

        upc, ufc, qkvc = _inproj(ctx, mod, g1, w_in, l, _ctx_mod_row, 512)
        up, uf, qkv, wo_l, w1_l, w3_l, w2_l = _inproj(x, mod, g1, w_in, l, _latent_mod_row, 1024,
                                                      cast=(w_out, w_ffn1, w_ffn3, w_ffn2))
        ffn_w = (wo_l, g2, w1_l, w3_l, w2_l)
        yp = _pool(up, pool_w_bd, scale, SEQ, SEQ)
        yf = _fourier_latent(uf, fourier_w_bd)
        at = _neighbourhood_attention(qkv, qkvc, _attention_bias(nat_bias[l]))
        x = _outffn(x, yp, yf, at, mod, *ffn_w, final_g[None, :] if last else None, _latent_mod_row, 1024)
        if not last:
            ypc = _pool(upc, pool_w_bd, scale, CTX_LEN, CTX_LEN)
            yfc = _fourier_ctx(ufc, fourier_w_bd)
            atc = _context_attention(qkvc)
            ctx = _outffn(ctx, ypc, yfc, atc, mod, *ffn_w, None, _ctx_mod_row, 512)
    return x.reshape(BATCH, SEQ, D_MODEL)
```

```python
import functools

import numpy as np
import jax
import jax.numpy as jnp
from jax import lax
from jax.experimental import pallas as pl
from jax.experimental.pallas import tpu as pltpu

D_MODEL = 1024
BATCH = 4
SEQ = 8192
DEPTH = 4
GRID_W = 64
GRID_ROWS = SEQ // GRID_W
CTX_LEN = 256
POOL_WINDOWS = (2, 4, 8, 16)
POOL_WIDTH = 256
FOURIER_WIDTH = 256
GROUP_WIDTH = 64
HEAD_DIM = 64
N_HEADS = 8
N_PAIRS = N_HEADS // 2
ATTN_WIDTH = 512
WIN_R = 8
WIN_C = 16
D_FF = 2816
EPS = 1e-6
NEG_INF = -1e30
LOG2_E = 1.4426950408889634
Q_SCALE = HEAD_DIM ** -0.5 * LOG2_E

FFT_N1 = 128
FFT_N2 = 64
POOL_HALO = 16
BIAS_SHAPE = (N_PAIRS, 2 * WIN_R - 2, 2 * GRID_W, 2 * GRID_W)

V7X_VMEM_LIMIT = 56 * 1024 * 1024
V7X_LANES = 128
PAIR_WIDTH = 2 * HEAD_DIM

F32 = jnp.float32
BF16 = jnp.bfloat16


def _dot(a, b):
    return jnp.dot(a, b, preferred_element_type=F32)


def _dot_nt(a, b):
    return lax.dot_general(a, b, (((1,), (1,)), ((), ())), preferred_element_type=F32)


def _params(n_grid_dims):
    return pltpu.CompilerParams(dimension_semantics=("arbitrary",) * n_grid_dims,
                                vmem_limit_bytes=V7X_VMEM_LIMIT)


def _resident(block_shape, index_map):
    return pl.BlockSpec(block_shape, index_map, pipeline_mode=pl.Buffered(1))


def _block_diag(m, groups):
    w = m.shape[0]
    out = np.zeros((groups * w, groups * w), m.dtype)
    for g in range(groups):
        out[g * w:(g + 1) * w, g * w:(g + 1) * w] = m
    return out


@functools.lru_cache(maxsize=None)
def _dft_constants():
    two_pi = 2.0 * np.pi
    a2 = two_pi * (np.outer(np.arange(FFT_N2), np.arange(FFT_N2)) % FFT_N2) / FFT_N2
    stage1 = np.stack([np.cos(a2), -np.sin(a2)], axis=1).reshape(2 * FFT_N2, FFT_N2)
    k = FFT_N2 * np.arange(FFT_N1)[None, :, None] + np.arange(FFT_N2)[:, None, None]
    n1 = np.arange(FFT_N1)[None, None, :]
    ang = two_pi * ((k * n1) % SEQ) / SEQ
    mr, mi = np.cos(ang), -np.sin(ang)
    stage2 = np.concatenate([np.concatenate([mr, -mi], axis=2),
                             np.concatenate([mi, mr], axis=2)], axis=1)
    ac = two_pi * (np.outer(np.arange(CTX_LEN), np.arange(CTX_LEN)) % CTX_LEN) / CTX_LEN
    ctx_dft = np.concatenate([np.cos(ac), -np.sin(ac)], axis=0)
    ag = two_pi * (np.outer(np.arange(GROUP_WIDTH), np.arange(GROUP_WIDTH)) % GROUP_WIDTH) / GROUP_WIDTH
    chan = np.concatenate([_block_diag(np.cos(ag), 4), _block_diag(np.sin(ag), 4)], axis=0)
    as_bf16 = lambda a: jnp.asarray(a, dtype=F32).astype(BF16)
    return as_bf16(stage1), as_bf16(stage2), as_bf16(ctx_dft), as_bf16(chan)


def _group_block_diag(w):
    out = jnp.zeros((4, GROUP_WIDTH, 4, GROUP_WIDTH), w.dtype)
    for g in range(4):
        out = out.at[g, :, g, :].set(w[g])
    return out.reshape(4 * GROUP_WIDTH, 4 * GROUP_WIDTH).astype(BF16)


def _attention_bias(table):
    col = np.arange(GRID_W)
    cs = np.clip(col - WIN_C // 2, 0, GRID_W - WIN_C)
    col_mask = (col[None, :] >= cs[:, None]) & (col[None, :] < cs[:, None] + WIN_C)
    dc_idx = np.clip(col[None, :] - col[:, None] + WIN_C - 1, 0, 2 * WIN_C - 2)
    onehot = (dc_idx[None] == np.arange(2 * WIN_C - 1)[:, None, None]).astype(np.float32)
    bt = jnp.einsum('hrd,dcj->hrcj', table.astype(F32), jnp.asarray(onehot),
                    precision=lax.Precision.HIGHEST)
    bt = jnp.where(col_mask[None, None], bt * LOG2_E, NEG_INF)
    b = jnp.concatenate([bt[:, :-1], bt[:, 1:]], axis=-1)
    b = b.reshape(N_PAIRS, 2, 2 * WIN_R - 2, GRID_W, 2 * GRID_W).transpose(0, 2, 1, 3, 4)
    return b.reshape(BIAS_SHAPE)


def _ada_kernel(c_ref, w_ref, b_ref, o_ref):
    c = c_ref[...]
    a = (c * jax.nn.sigmoid(c)).astype(BF16)
    o_ref[0] = _dot(a, w_ref[0].astype(BF16)) + b_ref[0]


def _ada_mod(cond, w_ada, b_ada):
    tn = 1536
    out = pl.pallas_call(
        _ada_kernel,
        grid=(DEPTH, 6 * D_MODEL // tn),
        in_specs=[_resident((8, D_MODEL), lambda l, j: (0, 0)),
                  pl.BlockSpec((1, D_MODEL, tn), lambda l, j: (l, 0, j)),
                  pl.BlockSpec((1, 1, tn), lambda l, j: (l, 0, j))],
        out_specs=pl.BlockSpec((1, 8, tn), lambda l, j: (l, 0, j)),
        out_shape=jax.ShapeDtypeStruct((DEPTH, 8, 6 * D_MODEL), F32),
        compiler_params=_params(2),
        name="ada_mod",
    )(cond, w_ada, b_ada.reshape(DEPTH, 1, 6 * D_MODEL))
    return out.reshape(DEPTH, 8, 6, D_MODEL)


def _rms(x):
    return x * lax.rsqrt(jnp.mean(x * x, axis=-1, keepdims=True) + EPS)


P_END = POOL_WIDTH
F_END = P_END + FOURIER_WIDTH
Q_END = F_END + ATTN_WIDTH
K_END = Q_END + ATTN_WIDTH
IN_WIDTH = K_END + ATTN_WIDTH
QKV_WIDTH = 3 * ATTN_WIDTH
INPROJ_PARTS = 4


def _inproj_kernel(x_ref, mod_ref, g_ref, w_ref, *refs):
    n_cast = (len(refs) - 3) // 2
    up_ref, uf_ref, qkv_ref = refs[n_cast:n_cast + 3]
    for src, dst in zip(refs[:n_cast], refs[n_cast + 3:]):
        dst[...] = src[...].astype(BF16)
    part_rows = x_ref.shape[0] // INPROJ_PARTS
    for part in range(INPROJ_PARTS):
        rows = slice(part * part_rows, (part + 1) * part_rows)
        y = _rms(x_ref[rows, :]) * g_ref[...]
        h = (y * (1.0 + mod_ref[0, 1:2, :]) + mod_ref[0, 0:1, :]).astype(BF16)
        up_ref[rows, :] = _dot(h, w_ref[:, 0:P_END]).astype(BF16)
        uf_ref[rows, :] = _dot(h, w_ref[:, P_END:F_END]).astype(BF16)
        qkv_ref[rows, 0:ATTN_WIDTH] = (_dot(h, w_ref[:, F_END:Q_END]) * Q_SCALE).astype(BF16)
        qkv_ref[rows, ATTN_WIDTH:2 * ATTN_WIDTH] = _dot(h, w_ref[:, Q_END:K_END]).astype(BF16)
        qkv_ref[rows, 2 * ATTN_WIDTH:QKV_WIDTH] = _dot(h, w_ref[:, K_END:IN_WIDTH]).astype(BF16)


def _layer_weight(layer, rows, cols):
    return _resident((None, rows, cols), lambda i: (layer, 0, 0))


BF16_SUBLANES = 16


def _inproj(x, mod, g, w_in, layer, mod_row, tm, cast=()):
    n = x.shape[0]
    steps = n // tm
    cast_in, cast_out, cast_shape = [], [], []
    for w in cast:
        _, rows, cols = w.shape
        share = next(k for k in (1, 2, 4, 8) if (rows * k) % (steps * BF16_SUBLANES) == 0)
        slab = rows * share // steps
        cast_in.append(pl.BlockSpec((None, slab, cols), lambda i, share=share: (layer, i // share, 0)))
        cast_out.append(pl.BlockSpec((slab, cols), lambda i, share=share: (i // share, 0)))
        cast_shape.append(jax.ShapeDtypeStruct((rows, cols), BF16))
    return pl.pallas_call(
        _inproj_kernel,
        grid=(steps,),
        in_specs=[pl.BlockSpec((tm, D_MODEL), lambda i: (i, 0)),
                  pl.BlockSpec((1, 6, D_MODEL), lambda i: (mod_row(i, tm), 0, 0)),
                  _resident((1, D_MODEL), lambda i: (0, 0)),
                  _layer_weight(layer, D_MODEL, IN_WIDTH)] + cast_in,
        out_specs=[pl.BlockSpec((tm, POOL_WIDTH), lambda i: (i, 0)),
                   pl.BlockSpec((tm, FOURIER_WIDTH), lambda i: (i, 0)),
                   pl.BlockSpec((tm, QKV_WIDTH), lambda i: (i, 0))] + cast_out,
        out_shape=[jax.ShapeDtypeStruct((n, POOL_WIDTH), BF16),
                   jax.ShapeDtypeStruct((n, FOURIER_WIDTH), BF16),
                   jax.ShapeDtypeStruct((n, QKV_WIDTH), BF16)] + cast_shape,
        compiler_params=_params(1),
        name="inproj",
    )(x, mod, g, w_in, *cast)


POOL_SUB = 256
POOL_BAND_K = POOL_SUB + 2 * POOL_HALO


@functools.lru_cache(maxsize=None)
def _pool_bands():
    t = np.arange(POOL_SUB)[:, None]
    d = np.arange(POOL_BAND_K)[None, :] - POOL_HALO - t
    member = [((d >= -(w // 2)) & (d < w - w // 2)).astype(np.float64) for w in POOL_WINDOWS]
    centred = [m / w - (d == 0) for m, w in zip(member, POOL_WINDOWS)]
    interior = np.stack([np.concatenate(centred[0:2], axis=1), np.concatenate(centred[2:4], axis=1)], axis=0)
    as_bf16 = lambda a: jnp.asarray(a, dtype=F32).astype(BF16)
    return as_bf16(np.concatenate(member, axis=0)), as_bf16(interior)


def _pool_kernel(band_ref, inner_ref, prev_ref, cur_ref, next_ref, pw_ref, ps_ref, o_ref, *, seq_tiles, nsub):
    st = pl.program_id(0) % seq_tiles
    ub_all = jnp.concatenate([prev_ref[...], cur_ref[...], next_ref[...]], axis=0)
    left = lax.broadcasted_iota(jnp.int32, (POOL_BAND_K, V7X_LANES), 1) < GROUP_WIDTH

    for s in range(nsub):
        ub = ub_all[s * POOL_SUB:s * POOL_SUB + POOL_BAND_K]
        cur = ub[POOL_HALO:POOL_HALO + POOL_SUB]

        def interior(ub=ub):
            halves = []
            for h in range(2):
                t = ub[:, V7X_LANES * h:V7X_LANES * (h + 1)]
                zero = jnp.zeros_like(t)
                stacked = jnp.concatenate([jnp.where(left, t, zero), jnp.where(left, zero, t)], axis=0)
                halves.append(_dot(inner_ref[h], stacked))
            return jnp.concatenate(halves, axis=1).astype(BF16)

        def clipped(ub=ub, cur=cur, s=s):
            lo_b = jnp.where(st == 0, 0, -POOL_HALO) if s == 0 else -POOL_HALO
            hi_b = (jnp.where(st == seq_tiles - 1, POOL_SUB, POOL_SUB + POOL_HALO) if s == nsub - 1
                    else POOL_SUB + POOL_HALO)
            p = lax.broadcasted_iota(jnp.int32, ub.shape, 0) - POOL_HALO
            ubv = jnp.where((p >= lo_b) & (p < hi_b), ub, jnp.zeros_like(ub))
            sums = _dot(band_ref[...], ubv)
            tcol = lax.broadcasted_iota(jnp.int32, (POOL_SUB, 1), 0)
            group = lax.broadcasted_iota(jnp.int32, (POOL_SUB, POOL_WIDTH), 1) // GROUP_WIDTH
            mean = jnp.zeros((POOL_SUB, POOL_WIDTH), F32)
            for g, w in enumerate(POOL_WINDOWS):
                cnt = jnp.minimum(tcol + (w - w // 2), hi_b) - jnp.maximum(tcol - w // 2, lo_b)
                mean = jnp.where(group == g, sums[g * POOL_SUB:(g + 1) * POOL_SUB] / cnt.astype(F32), mean)
            return (mean - cur.astype(F32)).astype(BF16)

        at_start = (st == 0) if s == 0 else None
        at_end = (st == seq_tiles - 1) if s == nsub - 1 else None
        if at_start is None and at_end is None:
            diff = interior()
        else:
            edge = at_start if at_end is None else (at_end if at_start is None else at_start | at_end)
            diff = lax.cond(edge, clipped, interior)
        o_ref[s * POOL_SUB:(s + 1) * POOL_SUB] = (_dot(diff, pw_ref[...]) * ps_ref[...]).astype(BF16)


def _pool(up, pool_w_bd, pool_scale, seq_len, tm):
    n = up.shape[0]
    hb = tm // POOL_HALO
    n_halo_blocks = n // POOL_HALO
    bands, inner = _pool_bands()
    return pl.pallas_call(
        functools.partial(_pool_kernel, seq_tiles=seq_len // tm, nsub=tm // POOL_SUB),
        grid=(n // tm,),
        in_specs=[_resident(bands.shape, lambda i: (0, 0)),
                  _resident(inner.shape, lambda i: (0, 0, 0)),
                  pl.BlockSpec((POOL_HALO, POOL_WIDTH), lambda i: (jnp.maximum(i * hb - 1, 0), 0)),
                  pl.BlockSpec((tm, POOL_WIDTH), lambda i: (i, 0)),
                  pl.BlockSpec((POOL_HALO, POOL_WIDTH),
                               lambda i: (jnp.minimum((i + 1) * hb, n_halo_blocks - 1), 0)),
                  _resident((POOL_WIDTH, POOL_WIDTH), lambda i: (0, 0)),
                  _resident((1, POOL_WIDTH), lambda i: (0, 0))],
        out_specs=pl.BlockSpec((tm, POOL_WIDTH), lambda i: (i, 0)),
        out_shape=jax.ShapeDtypeStruct((n, POOL_WIDTH), BF16),
        compiler_params=_params(1),
        name="pool",
    )(bands, inner, up, up, up, pool_w_bd, pool_scale)


FFT_CHUNK = 32


def _fft1_kernel(m_ref, x_ref, y_ref):
    xt = jnp.transpose(x_ref[...], (1, 0, 2))
    y = jnp.stack([_dot(m_ref[...], xt[n1]).astype(BF16) for n1 in range(FFT_CHUNK)], axis=0)
    y_ref[...] = jnp.transpose(y, (1, 0, 2)).reshape(y_ref.shape)


def _fourier_epilogue(zz, cs_ref, w_ref, norm):
    f = (_dot(zz, cs_ref[...]) * norm).astype(BF16)
    return _dot(f, w_ref[...])


def _fft2_kernel(t_ref, y_ref, cs_ref, w_ref, o_ref, zz, *, norm):
    for kk in range(FFT_CHUNK):
        z = _dot(t_ref[kk], y_ref[kk].reshape(2 * FFT_N1, FOURIER_WIDTH))
        zz[kk * FFT_N1:(kk + 1) * FFT_N1, :] = jnp.concatenate([z[:FFT_N1], z[FFT_N1:]], axis=1).astype(BF16)
    o = _fourier_epilogue(zz[...], cs_ref, w_ref, norm).astype(BF16)
    o_ref[...] = jnp.transpose(o.reshape(FFT_CHUNK, FFT_N1, FOURIER_WIDTH), (1, 0, 2))


def _fourier_latent(uf, fourier_w_bd):
    stage1, stage2, _, chan = _dft_constants()
    chunks1 = FFT_N1 // FFT_CHUNK
    y = pl.pallas_call(
        _fft1_kernel,
        grid=(BATCH, chunks1),
        in_specs=[_resident((2 * FFT_N2, FFT_N2), lambda b, j: (0, 0)),
                  pl.BlockSpec((FFT_N2, FFT_CHUNK, FOURIER_WIDTH), lambda b, j: (b, j, 0))],
        out_specs=pl.BlockSpec((FFT_N2, 2, FFT_CHUNK, FOURIER_WIDTH), lambda b, j: (b, 0, j, 0)),
        out_shape=jax.ShapeDtypeStruct((BATCH * FFT_N2, 2, FFT_N1, FOURIER_WIDTH), BF16),
        compiler_params=_params(2),
        name="fft_stage1",
    )(stage1, uf.reshape(BATCH * FFT_N2, FFT_N1, FOURIER_WIDTH))
    chunks2 = FFT_N2 // FFT_CHUNK
    out = pl.pallas_call(
        functools.partial(_fft2_kernel, norm=float((SEQ * GROUP_WIDTH) ** -0.5)),
        grid=(chunks2, BATCH),
        in_specs=[pl.BlockSpec((FFT_CHUNK, 2 * FFT_N1, 2 * FFT_N1), lambda j, b: (j, 0, 0)),
                  pl.BlockSpec((FFT_CHUNK, 2, FFT_N1, FOURIER_WIDTH), lambda j, b: (b * chunks2 + j, 0, 0, 0)),
                  _resident((2 * FOURIER_WIDTH, FOURIER_WIDTH), lambda j, b: (0, 0)),
                  _resident((FOURIER_WIDTH, FOURIER_WIDTH), lambda j, b: (0, 0))],
        out_specs=pl.BlockSpec((FFT_N1, FFT_CHUNK, FOURIER_WIDTH), lambda j, b: (b, j, 0)),
        out_shape=jax.ShapeDtypeStruct((BATCH * FFT_N1, FFT_N2, FOURIER_WIDTH), BF16),
        scratch_shapes=[pltpu.VMEM((FFT_CHUNK * FFT_N1, 2 * FOURIER_WIDTH), BF16)],
        compiler_params=_params(2),
        name="fft_stage2",
    )(stage2, y, chan, fourier_w_bd)
    return out.reshape(BATCH * SEQ, FOURIER_WIDTH)


def _fourier_ctx_kernel(m_ref, u_ref, cs_ref, w_ref, o_ref, *, norm):
    z = _dot(m_ref[...], u_ref[...])
    zz = jnp.concatenate([z[:CTX_LEN], z[CTX_LEN:]], axis=1).astype(BF16)
    o_ref[...] = _fourier_epilogue(zz, cs_ref, w_ref, norm).astype(BF16)


def _fourier_ctx(ufc, fourier_w_bd):
    _, _, ctx_dft, chan = _dft_constants()
    return pl.pallas_call(
        functools.partial(_fourier_ctx_kernel, norm=float((CTX_LEN * GROUP_WIDTH) ** -0.5)),
        grid=(BATCH,),
        in_specs=[_resident((2 * CTX_LEN, CTX_LEN), lambda b: (0, 0)),
                  pl.BlockSpec((CTX_LEN, FOURIER_WIDTH), lambda b: (b, 0)),
                  _resident((2 * FOURIER_WIDTH, FOURIER_WIDTH), lambda b: (0, 0)),
                  _resident((FOURIER_WIDTH, FOURIER_WIDTH), lambda b: (0, 0))],
        out_specs=pl.BlockSpec((CTX_LEN, FOURIER_WIDTH), lambda b: (b, 0)),
        out_shape=jax.ShapeDtypeStruct((BATCH * CTX_LEN, FOURIER_WIDTH), BF16),
        compiler_params=_params(1),
        name="fourier_ctx",
    )(ctx_dft, ufc, chan, fourier_w_bd)


def _split_pair(x, even):
    zero = jnp.zeros_like(x)
    return jnp.concatenate([jnp.where(even, x, zero), jnp.where(even, zero, x)], axis=0)


def _softmax_pv(scores, values, n_q, even):
    m = functools.reduce(jnp.maximum, [jnp.max(s, axis=1, keepdims=True) for s in scores])
    e = [jnp.exp2(s - m) for s in scores]
    l = functools.reduce(jnp.add, [jnp.sum(x, axis=1, keepdims=True) for x in e])
    o = functools.reduce(jnp.add, [_dot(x.astype(BF16), v) for x, v in zip(e, values)])
    o = o / l
    return jnp.where(even, o[:n_q], o[n_q:])


ATTN_TILE_ROWS = 64
ATTN_TILE = ATTN_TILE_ROWS * GRID_W
ATTN_HALO = WIN_R * GRID_W
ATTN_ROWS_PER_TRIP = 4
SOFTMAX_ROWS = 32


def _kv_window_row(t):
    return jnp.clip(t * ATTN_TILE_ROWS - WIN_R, 0, GRID_ROWS - ATTN_TILE_ROWS - 2 * WIN_R)


def _nat_kernel(q_ref, kbuf, vbuf, kx_ref, vx_ref, bias_ref, o_ref, sbuf_a, sbuf_b, pbuf):
    t = pl.program_id(1)
    even = lax.broadcasted_iota(jnp.int32, (GRID_W, PAIR_WIDTH), 1) < HEAD_DIM
    n_win = WIN_R * GRID_W

    def window_offset(j):
        r = t * ATTN_TILE_ROWS + j
        rs = jnp.clip(r - WIN_R // 2, 0, GRID_ROWS - WIN_R)
        return pl.multiple_of((rs - _kv_window_row(t)) * GRID_W, GRID_W)

    def query_offset(j):
        return j * GRID_W if isinstance(j, int) else pl.multiple_of(j * GRID_W, GRID_W)

    def scores(j, sbuf):
        r = t * ATTN_TILE_ROWS + j
        off = window_offset(j)
        dr0 = jnp.clip(r - WIN_R // 2, 0, GRID_ROWS - WIN_R) - r + WIN_R - 1
        qoff = query_offset(j)
        for p in range(N_PAIRS):
            lanes = slice(PAIR_WIDTH * p, PAIR_WIDTH * (p + 1))
            qs = _split_pair(q_ref[pl.ds(qoff, GRID_W), lanes], even)
            bias = jnp.concatenate([bias_ref[p, dr0 + 2 * m] for m in range(WIN_R // 2)], axis=1)
            sbuf[p, :, 0:n_win] = _dot_nt(qs, kbuf[pl.ds(off, n_win), lanes]) + bias
            sbuf[p, :, n_win:] = _dot_nt(qs, kx_ref[:, lanes])

    def outputs(j, sbuf):
        off = window_offset(j)
        qoff = query_offset(j)
        for p in range(N_PAIRS):
            lanes = slice(PAIR_WIDTH * p, PAIR_WIDTH * (p + 1))
            sums = []
            for g in range(2 * GRID_W // SOFTMAX_ROWS):
                rows = slice(g * SOFTMAX_ROWS, (g + 1) * SOFTMAX_ROWS)
                s = sbuf[p, rows, :]
                e = jnp.exp2(s - jnp.max(s, axis=1, keepdims=True))
                sums.append(jnp.sum(e, axis=1, keepdims=True))
                pbuf[p, rows, :] = e.astype(BF16)
            l = jnp.concatenate(sums, axis=0)
            o = (_dot(pbuf[p, :, :n_win], vbuf[pl.ds(off, n_win), lanes])
                 + _dot(pbuf[p, :, n_win:], vx_ref[:, lanes]))
            o = o / l
            o_ref[pl.ds(qoff, GRID_W), lanes] = jnp.where(even, o[:GRID_W], o[GRID_W:]).astype(BF16)

    scores(0, sbuf_a)

    def body(i, carry):
        base = ATTN_ROWS_PER_TRIP * i
        for k in range(ATTN_ROWS_PER_TRIP):
            cur, nxt = (sbuf_a, sbuf_b) if k % 2 == 0 else (sbuf_b, sbuf_a)
            scores(jnp.minimum(base + k + 1, ATTN_TILE_ROWS - 1), nxt)
            outputs(base + k, cur)
        return carry

    lax.fori_loop(0, ATTN_TILE_ROWS // ATTN_ROWS_PER_TRIP, body, 0)


def _neighbourhood_attention(qkv, qkv_ctx, bias):
    tiles = SEQ // ATTN_TILE
    cur = lambda col: (lambda b, t: (b * tiles + t, col))
    blk = lambda f: pl.BlockSpec((ATTN_TILE, ATTN_WIDTH), f)
    kv = lambda col: pl.BlockSpec(
        (pl.Element(ATTN_TILE + 2 * ATTN_HALO), pl.Element(ATTN_WIDTH)),
        lambda b, t: (pl.multiple_of(b * SEQ + _kv_window_row(t) * GRID_W, GRID_W), col * ATTN_WIDTH))
    return pl.pallas_call(
        _nat_kernel,
        grid=(BATCH, tiles),
        in_specs=[blk(cur(0)), kv(1), kv(2),
                  pl.BlockSpec((CTX_LEN, ATTN_WIDTH), lambda b, t: (b, 1)),
                  pl.BlockSpec((CTX_LEN, ATTN_WIDTH), lambda b, t: (b, 2)),
                  _resident(bias.shape, lambda b, t: (0, 0, 0, 0))],
        out_specs=blk(cur(0)),
        out_shape=jax.ShapeDtypeStruct((BATCH * SEQ, ATTN_WIDTH), BF16),
        scratch_shapes=[pltpu.VMEM((N_PAIRS, 2 * GRID_W, WIN_R * GRID_W + CTX_LEN), F32),
                        pltpu.VMEM((N_PAIRS, 2 * GRID_W, WIN_R * GRID_W + CTX_LEN), F32),
                        pltpu.VMEM((N_PAIRS, 2 * GRID_W, WIN_R * GRID_W + CTX_LEN), BF16)],
        compiler_params=_params(2),
        name="nat_attention",
    )(qkv, qkv, qkv, qkv_ctx, qkv_ctx, bias)


def _ctx_attn_kernel(q_ref, k_ref, v_ref, o_ref):
    even = lax.broadcasted_iota(jnp.int32, (CTX_LEN, PAIR_WIDTH), 1) < HEAD_DIM
    for p in range(N_PAIRS):
        lanes = slice(PAIR_WIDTH * p, PAIR_WIDTH * (p + 1))
        qs = _split_pair(q_ref[:, lanes], even)
        s = _dot_nt(qs, k_ref[:, lanes])
        o_ref[:, lanes] = _softmax_pv([s], [v_ref[:, lanes]], CTX_LEN, even).astype(BF16)


def _context_attention(qkv_ctx):
    blk = lambda col: pl.BlockSpec((CTX_LEN, ATTN_WIDTH), lambda b: (b, col))
    return pl.pallas_call(
        _ctx_attn_kernel,
        grid=(BATCH,),
        in_specs=[blk(0), blk(1), blk(2)],
        out_specs=blk(0),
        out_shape=jax.ShapeDtypeStruct((BATCH * CTX_LEN, ATTN_WIDTH), BF16),
        compiler_params=_params(1),
        name="ctx_attention",
    )(qkv_ctx, qkv_ctx, qkv_ctx)


V7X_MXU_WIDTH = 256
FFN_CHUNKS = ((0, 6 * V7X_MXU_WIDTH), (6 * V7X_MXU_WIDTH, D_FF))


def _outffn_kernel(x_ref, yp_ref, yf_ref, at_ref, mod_ref, wo_ref, g2_ref, w1_ref, w3_ref, w2_ref,
                   *rest, final):
    o_ref = rest[-1]
    y = jnp.concatenate([yp_ref[...], yf_ref[...], at_ref[...]], axis=1)
    x1 = x_ref[...] + mod_ref[0, 2:3, :] * _dot(y, wo_ref[...])
    h = (_rms(x1) * g2_ref[...] * (1.0 + mod_ref[0, 4:5, :]) + mod_ref[0, 3:4, :]).astype(BF16)
    ffn = jnp.zeros_like(x1)
    for lo, hi in FFN_CHUNKS:
        cols = slice(lo, hi)
        a = _dot(h, w1_ref[:, cols])
        gated = (a * jax.nn.sigmoid(a) * _dot(h, w3_ref[:, cols])).astype(BF16)
        ffn = ffn + _dot(gated, w2_ref[cols, :])
    x2 = x1 + mod_ref[0, 5:6, :] * ffn
    if final:
        x2 = _rms(x2) * rest[0][...]
    o_ref[...] = x2


def _outffn(x, yp, yf, at, mod, w_out, g2, w1, w3, w2, final_g, mod_row, tm):
    n = x.shape[0]
    row = lambda w: pl.BlockSpec((tm, w), lambda i: (i, 0))
    whole = lambda w: _resident(w.shape, lambda i: (0, 0))
    in_specs = [row(D_MODEL), row(POOL_WIDTH), row(FOURIER_WIDTH), row(ATTN_WIDTH),
                pl.BlockSpec((1, 6, D_MODEL), lambda i: (mod_row(i, tm), 0, 0)),
                whole(w_out), whole(g2), whole(w1), whole(w3), whole(w2)]
    args = [x, yp, yf, at, mod, w_out, g2, w1, w3, w2]
    if final_g is not None:
        in_specs.append(_resident((1, D_MODEL), lambda i: (0, 0)))
        args.append(final_g)
    return pl.pallas_call(
        functools.partial(_outffn_kernel, final=final_g is not None),
        grid=(n // tm,),
        in_specs=in_specs,
        out_specs=row(D_MODEL),
        out_shape=jax.ShapeDtypeStruct((n, D_MODEL), F32),
        compiler_params=_params(1),
        name="outproj_ffn",
    )(*args)


def _latent_mod_row(i, tm):
    return (i * tm) // SEQ


def _ctx_mod_row(i, tm):
    return BATCH


def kernel(x, c, ctx, c_ctx, w_ada, b_ada, norm1_g, w_in, pool_w, pool_scale, fourier_w, nat_bias, w_out,
           norm2_g, w_ffn1, w_ffn3, w_ffn2, final_g):
    cond = jnp.concatenate([c, c_ctx[None, :], jnp.zeros((8 - BATCH - 1, D_MODEL), F32)], axis=0)
    mods = _ada_mod(cond, w_ada, b_ada)
    x = x.reshape(BATCH * SEQ, D_MODEL)
    ctx = ctx.reshape(BATCH * CTX_LEN, D_MODEL)
    w_in = w_in.astype(BF16)
    for l in range(DEPTH):
        last = l == DEPTH - 1
        mod = mods[l]
        g1 = norm1_g[l][None, :]
        g2 = norm2_g[l][None, :]
        pool_w_bd = _group_block_diag(pool_w[l])
        fourier_w_bd = _group_block_diag(fourier_w[l])
        scale = pool_scale[l][None, :]

        upc, ufc, qkvc = _inproj(ctx, mod, g1, w_in, l, _ctx_mod_row, 512)
        up, uf, qkv, wo_l, w1_l, w3_l, w2_l = _inproj(x, mod, g1, w_in, l, _latent_mod_row, 1024,
                                                      cast=(w_out, w_ffn1, w_ffn3, w_ffn2))
        ffn_w = (wo_l, g2, w1_l, w3_l, w2_l)
        yp = _pool(up, pool_w_bd, scale, SEQ, SEQ)
        yf = _fourier_latent(uf, fourier_w_bd)
        at = _neighbourhood_attention(qkv, qkvc, _attention_bias(nat_bias[l]))
        x = _outffn(x, yp, yf, at, mod, *ffn_w, final_g[None, :] if last else None, _latent_mod_row, 1024)
        if not last:
            ypc = _pool(upc, pool_w_bd, scale, CTX_LEN, CTX_LEN)
            yfc = _fourier_ctx(ufc, fourier_w_bd)
            atc = _context_attention(qkvc)
            ctx = _outffn(ctx, ypc, yfc, atc, mod, *ffn_w, None, _ctx_mod_row, 512)
    return x.reshape(BATCH, SEQ, D_MODEL)
```
